```python
import jax
import jax.numpy as jnp
from jax import lax
import numpy as np

D_MODEL = 2048
BATCH = 8
SEQ = 2048
DEPTH = 2
DEC_BATCH = 128
DEC_SEQ = 1
PAST_LEN = 16384
PAGE_SIZE = 128

MLA_HEADS = 8
MLA_Q_RANK = 512
MLA_KV_RANK = 256
MLA_NOPE = 128
MLA_ROPE = 64
MLA_V = 128
MLA_QBLOCK = 128
MLA_SCALE = (MLA_NOPE + MLA_ROPE) ** -0.5
ROPE_THETA = 10000.0
HG_HEADS = 8
HG_K = 128
HG_V = 128
HG_CHUNK = 64
PEER_HEADS = 8
PEER_NKEYS = 128
PEER_EXPERTS = PEER_NKEYS * PEER_NKEYS
PEER_QDIM = 256
PEER_TOPK = 16
PEER_BLOCK = 128
EPS = 1e-6
IN_SIZES = (MLA_Q_RANK, MLA_KV_RANK, MLA_ROPE, HG_HEADS * HG_K, HG_HEADS * HG_K,
            HG_HEADS * HG_V, HG_HEADS * HG_V, D_MODEL, D_MODEL)
IN_WIDTH = sum(IN_SIZES)
F32 = jnp.float32

kernel_name = 'mla_hgrn2_peer_hybrid_step'


def _rmsnorm(x, g):
    xf = x.astype(F32)
    y = xf * lax.rsqrt(jnp.mean(xf * xf, axis=-1, keepdims=True) + EPS)
    return (y * g.astype(F32)).astype(x.dtype)


def _split(z, sizes):
    out, off = [], 0
    for s in sizes:
        out.append(z[..., off:off + s])
        off += s
    return out


def _rope(x, pos):
    half = x.shape[-1] // 2
    freq = ROPE_THETA ** (-jnp.arange(half, dtype=F32) / half)
    ang = pos.astype(F32)[:, None] * freq[None, :]
    shape = (1, ang.shape[0]) + (1,) * (x.ndim - 3) + (half,)
    cos = jnp.cos(ang).reshape(shape)
    sin = jnp.sin(ang).reshape(shape)
    x1 = x[..., :half].astype(F32)
    x2 = x[..., half:].astype(F32)
    return jnp.concatenate([x1 * cos - x2 * sin, x1 * sin + x2 * cos], axis=-1).astype(x.dtype)


def _prompt_attn(q_lat, q_rope, ckv, kr):
    b, t, h, c = q_lat.shape
    nb = t // MLA_QBLOCK
    ql = q_lat.astype(F32).reshape(b, nb, MLA_QBLOCK, h, c).swapaxes(0, 1)
    qr = q_rope.astype(F32).reshape(b, nb, MLA_QBLOCK, h, MLA_ROPE).swapaxes(0, 1)
    kc = ckv.astype(F32)
    kp = kr.astype(F32)
    kpos = jnp.arange(t)

    def block(args):
        qlb, qrb, i = args
        s = (jnp.einsum('bqhc,bkc->bhqk', qlb, kc) + jnp.einsum('bqhr,bkr->bhqk', qrb, kp)) * MLA_SCALE
        qpos = i * MLA_QBLOCK + jnp.arange(MLA_QBLOCK)
        s = jnp.where(kpos[None, :] <= qpos[:, None], s, -jnp.inf)
        p = jax.nn.softmax(s, axis=-1)
        return jnp.einsum('bhqk,bkc->bqhc', p, kc)

    out = lax.map(block, (ql, qr, jnp.arange(nb)))
    return out.swapaxes(0, 1).reshape(b, t, h, c)


def _online(carry, s, vals):
    m, l, acc = carry
    m_new = jnp.maximum(m, jnp.max(s, axis=-1))
    corr = jnp.exp(m - m_new)
    p = jnp.exp(s - m_new[..., None])
    return (m_new, l * corr + jnp.sum(p, axis=-1),
            acc * corr[..., None] + jnp.einsum('bqhk,bkc->bqhc', p, vals))


def _sample_attn(q_lat, q_rope, ckv_new, kr_new, cache_ckv, cache_krope, layer, page_table):
    b, tq, h, c = q_lat.shape
    ql = q_lat.astype(F32)
    qr = q_rope.astype(F32)

    def step(carry, pages):
        kc = cache_ckv[layer, pages].astype(F32)
        kp = cache_krope[layer, pages].astype(F32)
        s = (jnp.einsum('bqhc,bpc->bqhp', ql, kc) + jnp.einsum('bqhr,bpr->bqhp', qr, kp)) * MLA_SCALE
        return _online(carry, s, kc), None

    init = (jnp.full((b, tq, h), -jnp.inf, F32), jnp.zeros((b, tq, h), F32), jnp.zeros((b, tq, h, c), F32))
    carry, _ = lax.scan(step, init, page_table.T)
    kc = ckv_new.astype(F32)
    kp = kr_new.astype(F32)
    s = (jnp.einsum('bqhc,bkc->bqhk', ql, kc) + jnp.einsum('bqhr,bkr->bqhk', qr, kp)) * MLA_SCALE
    causal = jnp.tril(jnp.ones((tq, tq), dtype=bool))
    s = jnp.where(causal[None, :, None, :], s, -jnp.inf)
    m, l, acc = _online(carry, s, kc)
    return acc / l[..., None]


def _gla_chunk(s0, q, k, v, logf):
    s0 = s0.astype(F32)
    q = q.astype(F32)
    k = k.astype(F32)
    v = v.astype(F32)
    cum = jnp.cumsum(logf.astype(F32), axis=1)
    t = q.shape[1]
    causal = jnp.tril(jnp.ones((t, t), dtype=bool))[None, :, :, None, None]
    w = jnp.exp(jnp.where(causal, cum[:, :, None] - cum[:, None, :], -jnp.inf))
    a = jnp.einsum('bthk,btshk,bshk->bhts', q, w, k)
    o = jnp.einsum('bthk,bhkv->bthv', q * jnp.exp(cum), s0) + jnp.einsum('bhts,bshv->bthv', a, v)
    c_end = cum[:, -1]
    s_new = jnp.exp(c_end)[..., None] * s0 + jnp.einsum('bshk,bshv->bhkv', k * jnp.exp(c_end[:, None] - cum), v)
    return o, s_new


def _recur_prompt(q, k, v, logf):
    bsz, t, h, dk = q.shape
    nc = t // HG_CHUNK

    def chunks(a):
        return a.astype(F32).reshape(bsz, nc, HG_CHUNK, h, a.shape[-1]).swapaxes(0, 1)

    def step(s, xs):
        o, s = _gla_chunk(s, *xs)
        return s, o

    s0 = jnp.zeros((bsz, h, dk, HG_V), F32)
    s_end, o = lax.scan(step, s0, (chunks(q), chunks(k), chunks(v), chunks(logf)))
    return o.swapaxes(0, 1).reshape(bsz, t, h, HG_V), s_end


def _peer(h, wq, k1, k2, u, v):
    shp = h.shape
    hf = h.reshape(-1, D_MODEL)
    n = hf.shape[0]
    pad = (-n) % PEER_BLOCK
    hf = jnp.pad(hf, ((0, pad), (0, 0)))
    blocks = hf.reshape(-1, PEER_BLOCK, D_MODEL)
    half = PEER_QDIM // 2

    def one(xb):
        q = (xb @ wq).reshape(PEER_BLOCK, PEER_HEADS, PEER_QDIM).astype(F32)
        s1 = jnp.einsum('nhd,kd->nhk', q[..., :half], k1.astype(F32))
        s2 = jnp.einsum('nhd,kd->nhk', q[..., half:], k2.astype(F32))
        v1, i1 = lax.top_k(s1, PEER_TOPK)
        v2, i2 = lax.top_k(s2, PEER_TOPK)
        cand = (v1[..., :, None] + v2[..., None, :]).reshape(PEER_BLOCK, PEER_HEADS, PEER_TOPK * PEER_TOPK)
        cidx = (i1[..., :, None] * PEER_NKEYS + i2[..., None, :]).reshape(PEER_BLOCK, PEER_HEADS, PEER_TOPK * PEER_TOPK)
        sc, sel = lax.top_k(cand, PEER_TOPK)
        eidx = jnp.take_along_axis(cidx, sel, axis=-1)
        g = jax.nn.softmax(sc, axis=-1)
        ue = u[eidx]
        ve = v[eidx]
        act = jax.nn.gelu(jnp.einsum('nd,nhkd->nhk', xb, ue).astype(F32), approximate=False)
        return jnp.einsum('nhk,nhkd->nd', (g * act).astype(xb.dtype), ve)

    out = lax.map(one, blocks).reshape(-1, D_MODEL)[:n]
    return out.reshape(shp)


def _trunk(x, c, pos, attend, recur, p):
    lbs = jnp.cumsum(jax.nn.softmax(p['hg_lb_logits'].astype(F32), axis=0), axis=0)
    lbs = lbs - lbs[0]
    bsz, t, _ = x.shape
    cs = jax.nn.silu(c)
    ckvs, krs, states = [], [], []
    for l in range(DEPTH):
        mod = (cs @ p['w_ada'][l] + p['b_ada'][l])[:, None, :]
        sh1, sc1, gt1, sh2, sc2, gt2 = jnp.split(mod, 6, axis=-1)
        h = _rmsnorm(x, p['g_norm1'][l]) * (1 + sc1) + sh1
        cq, ckv, kr, hq, hf, hi, hg, ga, gb = _split(h @ p['w_in'][l], IN_SIZES)
        q = (_rmsnorm(cq, p['g_qnorm'][l]) @ p['w_uq'][l]).reshape(bsz, t, MLA_HEADS, MLA_NOPE + MLA_ROPE)
        q_lat = jnp.einsum('bthd,chd->bthc', q[..., :MLA_NOPE], p['w_uk'][l])
        q_rope = _rope(q[..., MLA_NOPE:], pos)
        ckv = _rmsnorm(ckv, p['g_kvnorm'][l])
        kr = _rope(kr, pos)
        lat = attend(l, q_lat, q_rope, ckv, kr).astype(x.dtype)
        o_a = jnp.einsum('bthc,chv->bthv', lat, p['w_uv'][l]).reshape(bsz, t, MLA_HEADS * MLA_V) @ p['w_a'][l]
        lb = lbs[l].reshape(HG_HEADS, HG_K)
        fp = hf.astype(F32).reshape(bsz, t, HG_HEADS, HG_K)
        logf = jnp.logaddexp(jnp.log(lb), jnp.log1p(-lb) + jax.nn.log_sigmoid(fp))
        kk = (1 - lb) * jax.nn.sigmoid(-fp)
        o_r, s_end = recur(l, hq.reshape(bsz, t, HG_HEADS, HG_K), kk, hi.reshape(bsz, t, HG_HEADS, HG_V), logf)
        o_r = _rmsnorm(o_r.astype(x.dtype), p['g_hg_onorm'][l]) * jax.nn.silu(hg.reshape(bsz, t, HG_HEADS, HG_V))
        o_b = o_r.reshape(bsz, t, HG_HEADS * HG_V) @ p['w_b'][l]
        y = (jax.nn.sigmoid(ga) * o_a + jax.nn.sigmoid(gb) * o_b) @ p['w_out'][l]
        x = x + gt1 * y
        h2 = _rmsnorm(x, p['g_norm2'][l]) * (1 + sc2) + sh2
        x = x + gt2 * _peer(h2, p['peer_wq'][l], p['peer_k1'][l], p['peer_k2'][l], p['peer_u'][l], p['peer_v'][l])
        ckvs.append(ckv)
        krs.append(kr)
        states.append(s_end.astype(x.dtype))
    return _rmsnorm(x, p['g_final']), jnp.stack(ckvs), jnp.stack(krs), jnp.stack(states)


def setup_inputs(seed: int = 0) -> dict:
    key = jax.random.key(seed)
    ks = iter(jax.random.split(key, 40))

    def nrm(shape, scale):
        return jax.random.normal(next(ks), shape, F32) * scale

    def gain(shape):
        return 1.0 + nrm(shape, 0.02)

    n_pages = PAST_LEN // PAGE_SIZE
    n_pool = (5 * DEC_BATCH * n_pages + 3) // 4
    perm = jax.random.permutation(next(ks), n_pool)
    page_table = perm[:DEC_BATCH * n_pages].reshape(DEC_BATCH, n_pages).astype(jnp.int32)
    return {
        'x_prompt': nrm((BATCH, SEQ, D_MODEL), 1.0),
        'x_sample': nrm((DEC_BATCH, DEC_SEQ, D_MODEL), 1.0),
        'cache_ckv': nrm((DEPTH, n_pool, PAGE_SIZE, MLA_KV_RANK), 1.0),
        'cache_krope': nrm((DEPTH, n_pool, PAGE_SIZE, MLA_ROPE), 1.0),
        'state_hgrn': nrm((DEPTH, DEC_BATCH, HG_HEADS, HG_K, HG_V), 0.5),
        'page_table': page_table,
        'c_prompt': nrm((BATCH, D_MODEL), 1.0),
        'c_sample': nrm((DEC_BATCH, D_MODEL), 1.0),
        'w_ada': nrm((DEPTH, D_MODEL, 6 * D_MODEL), 0.5 * D_MODEL ** -0.5),
        'b_ada': nrm((DEPTH, 6 * D_MODEL), 0.02),
        'g_norm1': gain((DEPTH, D_MODEL)),
        'w_in': nrm((DEPTH, D_MODEL, IN_WIDTH), D_MODEL ** -0.5),
        'g_qnorm': gain((DEPTH, MLA_Q_RANK)),
        'w_uq': nrm((DEPTH, MLA_Q_RANK, MLA_HEADS * (MLA_NOPE + MLA_ROPE)), MLA_Q_RANK ** -0.5),
        'g_kvnorm': gain((DEPTH, MLA_KV_RANK)),
        'w_uk': nrm((DEPTH, MLA_KV_RANK, MLA_HEADS, MLA_NOPE), MLA_KV_RANK ** -0.5),
        'w_uv': nrm((DEPTH, MLA_KV_RANK, MLA_HEADS, MLA_V), MLA_KV_RANK ** -0.5),
        'w_a': nrm((DEPTH, MLA_HEADS * MLA_V, D_MODEL), (MLA_HEADS * MLA_V) ** -0.5),
        'hg_lb_logits': nrm((DEPTH, HG_HEADS * HG_K), 0.5),
        'g_hg_onorm': gain((DEPTH, HG_V)),
        'w_b': nrm((DEPTH, HG_HEADS * HG_V, D_MODEL), (HG_HEADS * HG_V) ** -0.5),
        'w_out': nrm((DEPTH, D_MODEL, D_MODEL), D_MODEL ** -0.5),
        'g_norm2': gain((DEPTH, D_MODEL)),
        'peer_wq': nrm((DEPTH, D_MODEL, PEER_HEADS * PEER_QDIM), D_MODEL ** -0.5),
        'peer_k1': nrm((DEPTH, PEER_NKEYS, PEER_QDIM // 2), (PEER_QDIM // 2) ** -0.5),
        'peer_k2': nrm((DEPTH, PEER_NKEYS, PEER_QDIM // 2), (PEER_QDIM // 2) ** -0.5),
        'peer_u': nrm((DEPTH, PEER_EXPERTS, D_MODEL), D_MODEL ** -0.5),
        'peer_v': nrm((DEPTH, PEER_EXPERTS, D_MODEL), 0.5),
        'g_final': gain((D_MODEL,)),
    }


def reference(x_prompt, x_sample, cache_ckv, cache_krope, state_hgrn, page_table, c_prompt, c_sample,
              w_ada, b_ada, g_norm1, w_in, g_qnorm, w_uq, g_kvnorm, w_uk, w_uv, w_a, hg_lb_logits,
              g_hg_onorm, w_b, w_out, g_norm2, peer_wq, peer_k1, peer_k2, peer_u, peer_v, g_final):
    p = {'w_ada': w_ada, 'b_ada': b_ada, 'g_norm1': g_norm1, 'w_in': w_in, 'g_qnorm': g_qnorm,
         'w_uq': w_uq, 'g_kvnorm': g_kvnorm, 'w_uk': w_uk, 'w_uv': w_uv, 'w_a': w_a,
         'hg_lb_logits': hg_lb_logits, 'g_hg_onorm': g_hg_onorm, 'w_b': w_b, 'w_out': w_out,
         'g_norm2': g_norm2, 'peer_wq': peer_wq, 'peer_k1': peer_k1, 'peer_k2': peer_k2,
         'peer_u': peer_u, 'peer_v': peer_v, 'g_final': g_final}
    pos_p = jnp.arange(x_prompt.shape[1])
    y_prompt, ckv_prompt, krope_prompt, state_prompt = _trunk(
        x_prompt, c_prompt, pos_p,
        lambda l, ql, qr, ckv, kr: _prompt_attn(ql, qr, ckv, kr),
        lambda l, q, k, v, f: _recur_prompt(q, k, v, f), p)
    pos_s = PAST_LEN + jnp.arange(x_sample.shape[1])
    y_sample, ckv_sample, krope_sample, state_sample = _trunk(
        x_sample, c_sample, pos_s,
        lambda l, ql, qr, ckv, kr: _sample_attn(ql, qr, ckv, kr, cache_ckv, cache_krope, l, page_table),
        lambda l, q, k, v, f: _gla_chunk(state_hgrn[l], q, k, v, f), p)
    return (y_prompt, y_sample, ckv_prompt, krope_prompt, state_prompt, ckv_sample, krope_sample, state_sample)
```

```python
import functools

import jax
import jax.numpy as jnp
from jax import lax
from jax.experimental import pallas as pl
from jax.experimental.pallas import tpu as pltpu

F32 = jnp.float32
BF16 = jnp.bfloat16

MLA_HEADS = 8
MLA_Q_RANK = 512
MLA_KV_RANK = 256
MLA_NOPE = 128
MLA_ROPE = 64
MLA_V = 128
MLA_SCALE = (MLA_NOPE + MLA_ROPE) ** -0.5
ROPE_THETA = 10000.0
HG_HEADS = 8
HG_K = 128
HG_V = 128
PEER_HEADS = 8
PEER_NKEYS = 128
PEER_QDIM = 256
PEER_TOPK = 16
EPS = 1e-6

QK_PAD = 384
HG_CHUNK = 64
HG_SUB = 16
VMEM_LIMIT = 56 * 1024 * 1024
ATT_TQ, ATT_TK, ATT_GP = 256, 512, 32

COL_GA, COL_GB, COL_A, COL_HQ, COL_HF, COL_HI, COL_HG, IN_PAD = 0, 2048, 4096, 5120, 6144, 7168, 8192, 9216


def _cparams(sem):
    return pltpu.CompilerParams(dimension_semantics=sem, vmem_limit_bytes=VMEM_LIMIT)


def _tile(n, pref):
    t = min(n, pref)
    while n % t:
        t -= 8
    return t


def _dot(a, b):
    return jnp.dot(a, b, preferred_element_type=F32)


def _dot_nt(a, b):
    return lax.dot_general(a, b, (((1,), (1,)), ((), ())), preferred_element_type=F32)


def _dot_tn(a, b):
    return lax.dot_general(a, b, (((0,), (0,)), ((), ())), preferred_element_type=F32)


def _split3(x):
    hi = x.astype(BF16)
    r = x - hi.astype(F32)
    mid = r.astype(BF16)
    lo = (r - mid.astype(F32)).astype(BF16)
    return hi, mid, lo


def _rms(x, g):
    return x * lax.rsqrt(jnp.mean(x * x, axis=-1, keepdims=True) + EPS) * g


def _const_spec(shape):
    nd = len(shape)
    return pl.BlockSpec(shape, lambda *_: (0,) * nd)


def _ada_kernel(c_ref, w_ref, b_ref, o_ref):
    c = c_ref[...]
    cs = c * jax.nn.sigmoid(c)
    a_hi, a_mid, _ = _split3(cs)
    w_hi, w_mid, _ = _split3(w_ref[0])
    o_ref[0] = _dot(a_hi, w_hi) + _dot(a_mid, w_hi) + _dot(a_hi, w_mid) + b_ref[0]


def _ada_mod(c_all, w_ada, b_ada):
    depth, d, w6 = w_ada.shape
    r = c_all.shape[0]
    tn = 1024
    return pl.pallas_call(
        _ada_kernel,
        grid=(depth, w6 // tn),
        in_specs=[
            pl.BlockSpec((r, d), lambda l, j: (0, 0)),
            pl.BlockSpec((1, d, tn), lambda l, j: (l, 0, j)),
            pl.BlockSpec((1, 1, tn), lambda l, j: (l, 0, j)),
        ],
        out_specs=pl.BlockSpec((1, r, tn), lambda l, j: (l, 0, j)),
        out_shape=jax.ShapeDtypeStruct((depth, r, w6), F32),
        compiler_params=_cparams(("parallel", "parallel")),
        name="ada_mod",
    )(c_all, w_ada, b_ada.reshape(depth, 1, w6))


def _in_proj_kernel(x_ref, sc_ref, sh_ref, g_ref, w_ref, o_ref, h_scr):
    @pl.when(pl.program_id(1) == 0)
    def _():
        h = _rms(x_ref[...], g_ref[...]) * (1.0 + sc_ref[0]) + sh_ref[0]
        h_scr[...] = h.astype(BF16)

    o_ref[...] = _dot(h_scr[...], w_ref[...])


def _in_proj(x, mod3, g1, w_in_p, st):
    n, d = x.shape
    tm, rpg, r = st["tm"], st["rpg"], st["r"]
    tn = 1024
    width = w_in_p.shape[1]

    def gi(i):
        return (i * tm) // rpg

    return pl.pallas_call(
        _in_proj_kernel,
        grid=(n // tm, width // tn),
        in_specs=[
            pl.BlockSpec((tm, d), lambda i, j: (i, 0)),
            pl.BlockSpec((1, r, d), lambda i, j: (gi(i), 0, 1)),
            pl.BlockSpec((1, r, d), lambda i, j: (gi(i), 0, 0)),
            pl.BlockSpec((1, d), lambda i, j: (0, 0)),
            pl.BlockSpec((d, tn), lambda i, j: (0, j)),
        ],
        out_specs=pl.BlockSpec((tm, tn), lambda i, j: (i, j)),
        out_shape=jax.ShapeDtypeStruct((n, width), F32),
        scratch_shapes=[pltpu.VMEM((tm, d), BF16)],
        compiler_params=_cparams(("parallel", "arbitrary")),
        name="in_proj",
    )(x, mod3, mod3, g1, w_in_p)


def _mla_prep_kernel(z_ref, gq_ref, gkv_ref, wn_ref, wr_ref, wrs_ref, wuk_ref, cos_ref, sin_ref,
                     q_ref, ckv_ref, kr_ref, kvb_ref):
    z = z_ref[...]
    cq = z[:, :MLA_Q_RANK]
    ckv = z[:, MLA_Q_RANK:MLA_Q_RANK + MLA_KV_RANK]
    o = MLA_Q_RANK + MLA_KV_RANK
    kr = z[:, o:o + MLA_ROPE]
    krs = z[:, o + MLA_ROPE:o + 2 * MLA_ROPE]
    cos = cos_ref[...]
    sin = sin_ref[...]

    cqn = _rms(cq, gq_ref[...]).astype(BF16)
    qn = _dot(cqn, wn_ref[...])
    qrope = (_dot(cqn, wr_ref[...]) * cos + _dot(cqn, wrs_ref[...]) * sin) * MLA_SCALE
    tm = z.shape[0]
    zpad = jnp.zeros((tm, QK_PAD - MLA_KV_RANK - MLA_ROPE), BF16)
    for h in range(MLA_HEADS):
        ql = _dot(qn[:, h * MLA_NOPE:(h + 1) * MLA_NOPE].astype(BF16), wuk_ref[h]) * MLA_SCALE
        q_ref[h, :, 0:MLA_KV_RANK] = ql.astype(BF16)
        q_ref[h, :, MLA_KV_RANK:MLA_KV_RANK + MLA_ROPE] = qrope[:, h * MLA_ROPE:(h + 1) * MLA_ROPE].astype(BF16)
        q_ref[h, :, MLA_KV_RANK + MLA_ROPE:QK_PAD] = zpad

    ckvn = _rms(ckv, gkv_ref[...])
    kro = kr * cos[:, :MLA_ROPE] + krs * sin[:, :MLA_ROPE]
    ckv_ref[...] = ckvn
    kr_ref[...] = kro
    kvb_ref[:, 0:MLA_KV_RANK] = ckvn.astype(BF16)
    kvb_ref[:, MLA_KV_RANK:MLA_KV_RANK + MLA_ROPE] = kro.astype(BF16)
    kvb_ref[:, MLA_KV_RANK + MLA_ROPE:QK_PAD] = zpad


def _mla_prep(z, lw, cos_t, sin_t, st):
    n = z.shape[0]
    tm = st["tm"]
    rt = cos_t.shape[0]
    hr = MLA_HEADS * MLA_ROPE
    if rt == 1:
        tspec = pl.BlockSpec((1, hr), lambda i: (0, 0))
    else:
        nt = rt // tm
        tspec = pl.BlockSpec((tm, hr), lambda i: (i % nt, 0))
    return pl.pallas_call(
        _mla_prep_kernel,
        grid=(n // tm,),
        in_specs=[
            pl.BlockSpec((tm, 1024), lambda i: (i, COL_A // 1024)),
            _const_spec((1, MLA_Q_RANK)),
            _const_spec((1, MLA_KV_RANK)),
            _const_spec((MLA_Q_RANK, MLA_HEADS * MLA_NOPE)),
            _const_spec((MLA_Q_RANK, hr)),
            _const_spec((MLA_Q_RANK, hr)),
            _const_spec((MLA_HEADS, MLA_NOPE, MLA_KV_RANK)),
            tspec,
            tspec,
        ],
        out_specs=[
            pl.BlockSpec((MLA_HEADS, tm, QK_PAD), lambda i: (0, i, 0)),
            pl.BlockSpec((tm, MLA_KV_RANK), lambda i: (i, 0)),
            pl.BlockSpec((tm, MLA_ROPE), lambda i: (i, 0)),
            pl.BlockSpec((tm, QK_PAD), lambda i: (i, 0)),
        ],
        out_shape=[
            jax.ShapeDtypeStruct((MLA_HEADS, n, QK_PAD), BF16),
            jax.ShapeDtypeStruct((n, MLA_KV_RANK), F32),
            jax.ShapeDtypeStruct((n, MLA_ROPE), F32),
            jax.ShapeDtypeStruct((n, QK_PAD), BF16),
        ],
        compiler_params=_cparams(("parallel",)),
        name="mla_prep",
    )(z, lw["g_q"], lw["g_kv"], lw["w_nope"], lw["w_rope"], lw["w_rope_sw"], lw["w_ukT"], cos_t, sin_t)


def _attn_prompt_kernel(q_ref, k_ref, wuv_ref, o_ref, m_scr, l_scr, acc_scr, *, tq, tk):
    i = pl.program_id(1)
    j = pl.program_id(2)
    nh = MLA_HEADS

    @pl.when(j == 0)
    def _():
        m_scr[...] = jnp.full(m_scr.shape, -jnp.inf, F32)
        l_scr[...] = jnp.zeros(l_scr.shape, F32)
        acc_scr[...] = jnp.zeros(acc_scr.shape, F32)

    @pl.when(j * tk <= i * tq + tq - 1)
    def _():
        q = q_ref[...].reshape(nh * tq, QK_PAD)
        k = k_ref[...]
        s = _dot_nt(q, k)
        qpos = i * tq + lax.broadcasted_iota(jnp.int32, (nh, tq, tk), 1).reshape(nh * tq, tk)
        kpos = j * tk + lax.broadcasted_iota(jnp.int32, (nh * tq, tk), 1)
        s = jnp.where(kpos <= qpos, s, -jnp.inf)
        m_prev = m_scr[...]
        m_new = jnp.maximum(m_prev, jnp.max(s, axis=-1, keepdims=True))
        corr = jnp.exp(m_prev - m_new)
        p = jnp.exp(s - m_new)
        l_scr[...] = l_scr[...] * corr + jnp.sum(p, axis=-1, keepdims=True)
        acc_scr[...] = acc_scr[...] * corr + _dot(p.astype(BF16), k[:, :MLA_KV_RANK])
        m_scr[...] = m_new

    @pl.when(j == pl.num_programs(2) - 1)
    def _():
        lat = acc_scr[...] / l_scr[...]
        for h in range(nh):
            o_ref[:, h * MLA_V:(h + 1) * MLA_V] = _dot(
                lat[h * tq:(h + 1) * tq].astype(BF16), wuv_ref[h]).astype(BF16)


def _attn_prompt(q, kvb, wuv, bsz, t):
    n = kvb.shape[0]
    tq = _tile(t, ATT_TQ)
    tk = _tile(t, ATT_TK)
    nq, nk = t // tq, t // tk
    kern = functools.partial(_attn_prompt_kernel, tq=tq, tk=tk)
    return pl.pallas_call(
        kern,
        grid=(bsz, nq, nk),
        in_specs=[
            pl.BlockSpec((MLA_HEADS, tq, QK_PAD), lambda b, i, j: (0, b * nq + i, 0)),
            pl.BlockSpec((tk, QK_PAD), lambda b, i, j: (b * nk + jnp.minimum(j, (i * tq + tq - 1) // tk), 0)),
            _const_spec((MLA_HEADS, MLA_KV_RANK, MLA_V)),
        ],
        out_specs=pl.BlockSpec((tq, MLA_HEADS * MLA_V), lambda b, i, j: (b * nq + i, 0)),
        out_shape=jax.ShapeDtypeStruct((n, MLA_HEADS * MLA_V), BF16),
        scratch_shapes=[
            pltpu.VMEM((MLA_HEADS * tq, 1), F32),
            pltpu.VMEM((MLA_HEADS * tq, 1), F32),
            pltpu.VMEM((MLA_HEADS * tq, MLA_KV_RANK), F32),
        ],
        compiler_params=_cparams(("parallel", "parallel", "arbitrary")),
        name="attn_prompt",
    )(q, kvb, wuv)


def _attn_sample_kernel(pt_ref, q_ref, knew_ref, ckv_hbm, kr_hbm, o_ref,
                        kbuf, rbuf, sem, m_scr, l_scr, acc_scr, *, layer, gp, ng, npg, nsteps):
    b = pl.program_id(0)
    g = pl.program_id(1)
    step = b * ng + g
    slot = step % 2

    def copies(st, sl):
        base = (st // ng) * npg + (st % ng) * gp
        cps = []
        for k in range(gp):
            page = pt_ref[base + k]
            cps.append(pltpu.make_async_copy(ckv_hbm.at[layer, page], kbuf.at[sl, k], sem.at[0, sl]))
            cps.append(pltpu.make_async_copy(kr_hbm.at[layer, page], rbuf.at[sl, k], sem.at[1, sl]))
        return cps

    @pl.when(step == 0)
    def _():
        for c in copies(step, slot):
            c.start()

    @pl.when(step + 1 < nsteps)
    def _():
        for c in copies(step + 1, 1 - slot):
            c.start()

    for c in copies(step, slot):
        c.wait()

    @pl.when(g == 0)
    def _():
        m_scr[...] = jnp.full(m_scr.shape, -jnp.inf, F32)
        l_scr[...] = jnp.zeros(l_scr.shape, F32)
        acc_scr[...] = jnp.zeros(acc_scr.shape, F32)

    q = q_ref[0]
    ps = kbuf.shape[2]
    kc = kbuf[slot].reshape(gp * ps, MLA_KV_RANK).astype(BF16)
    kp = rbuf[slot].reshape(gp * ps, MLA_ROPE).astype(BF16)
    s = _dot_nt(q[:, :MLA_KV_RANK], kc) + _dot_nt(q[:, MLA_KV_RANK:MLA_KV_RANK + MLA_ROPE], kp)
    m_prev = m_scr[...]
    m_new = jnp.maximum(m_prev, jnp.max(s, axis=-1, keepdims=True))
    corr = jnp.exp(m_prev - m_new)
    p = jnp.exp(s - m_new)
    l_scr[...] = l_scr[...] * corr + jnp.sum(p, axis=-1, keepdims=True)
    acc_scr[...] = acc_scr[...] * corr + _dot(p.astype(BF16), kc)
    m_scr[...] = m_new

    @pl.when(g == ng - 1)
    def _():
        knew = knew_ref[0].astype(F32)
        s_new = jnp.sum(q.astype(F32) * knew, axis=-1, keepdims=True)
        m_old = m_scr[...]
        m_fin = jnp.maximum(m_old, s_new)
        c_old = jnp.exp(m_old - m_fin)
        p_new = jnp.exp(s_new - m_fin)
        l_fin = l_scr[...] * c_old + p_new
        acc = acc_scr[...] * c_old + p_new * knew[:, :MLA_KV_RANK]
        o_ref[0] = acc / l_fin


def _attn_sample(q_nh, kvb, cache_ckv, cache_krope, page_table, layer):
    n = q_nh.shape[0]
    npg = page_table.shape[1]
    ps = cache_ckv.shape[2]
    gp = min(npg, ATT_GP)
    while npg % gp:
        gp -= 1
    ng = npg // gp
    kern = functools.partial(_attn_sample_kernel, layer=layer, gp=gp, ng=ng, npg=npg, nsteps=n * ng)
    grid_spec = pltpu.PrefetchScalarGridSpec(
        num_scalar_prefetch=1,
        grid=(n, ng),
        in_specs=[
            pl.BlockSpec((1, MLA_HEADS, QK_PAD), lambda b, g, pt: (b, 0, 0)),
            pl.BlockSpec((1, 1, QK_PAD), lambda b, g, pt: (b, 0, 0)),
            pl.BlockSpec(memory_space=pl.ANY),
            pl.BlockSpec(memory_space=pl.ANY),
        ],
        out_specs=pl.BlockSpec((1, MLA_HEADS, MLA_KV_RANK), lambda b, g, pt: (b, 0, 0)),
        scratch_shapes=[
            pltpu.VMEM((2, gp, ps, MLA_KV_RANK), F32),
            pltpu.VMEM((2, gp, ps, MLA_ROPE), F32),
            pltpu.SemaphoreType.DMA((2, 2)),
            pltpu.VMEM((MLA_HEADS, 1), F32),
            pltpu.VMEM((MLA_HEADS, 1), F32),
            pltpu.VMEM((MLA_HEADS, MLA_KV_RANK), F32),
        ],
    )
    return pl.pallas_call(
        kern,
        grid_spec=grid_spec,
        out_shape=jax.ShapeDtypeStruct((n, MLA_HEADS, MLA_KV_RANK), F32),
        compiler_params=_cparams(("arbitrary", "arbitrary")),
        name="attn_sample",
    )(page_table.reshape(-1), q_nh, kvb.reshape(n, 1, QK_PAD), cache_ckv, cache_krope)


def _uv_proj_kernel(lat_ref, wuv_ref, o_ref):
    for h in range(MLA_HEADS):
        o_ref[:, h * MLA_V:(h + 1) * MLA_V] = _dot(lat_ref[h].astype(BF16), wuv_ref[h]).astype(BF16)


def _uv_proj(lat_hn, wuv):
    n = lat_hn.shape[1]
    return pl.pallas_call(
        _uv_proj_kernel,
        grid=(1,),
        in_specs=[_const_spec(lat_hn.shape), _const_spec(wuv.shape)],
        out_specs=_const_spec((n, MLA_HEADS * MLA_V)),
        out_shape=jax.ShapeDtypeStruct((n, MLA_HEADS * MLA_V), BF16),
        compiler_params=_cparams(("arbitrary",)),
        name="uv_proj",
    )(lat_hn, wuv)


def _hg_gates(fp, lb_ref):
    lsig = jnp.minimum(fp, 0.0) - jnp.log1p(jnp.exp(-jnp.abs(fp)))
    bterm = lb_ref[1:2, :] + lsig
    loglb = lb_ref[0:1, :]
    logf = jnp.maximum(loglb, bterm) + jnp.log1p(jnp.exp(-jnp.abs(loglb - bterm)))
    kk = lb_ref[2:3, :] * jax.nn.sigmoid(-fp)
    return logf, kk


def _hgrn_prompt_kernel(q_ref, f_ref, i_ref, lb_ref, o_ref, s_ref, st_scr, *, c, sb):
    ci = pl.program_id(1)
    nsb = c // sb
    kd = HG_K

    @pl.when(ci == 0)
    def _():
        st_scr[...] = jnp.zeros(st_scr.shape, F32)

    logf, kk_all = _hg_gates(f_ref[...], lb_ref)
    tril = (lax.broadcasted_iota(jnp.int32, (c, c), 0) >= lax.broadcasted_iota(jnp.int32, (c, c), 1)).astype(BF16)
    l_hi, l_mid, l_lo = _split3(logf)
    cum_all = _dot(tril, l_hi) + _dot(tril, l_mid) + _dot(tril, l_lo)
    q_all = q_ref[...]
    v_all = i_ref[...]
    ones = jnp.ones((kd, kd), BF16)
    tpos = lax.broadcasted_iota(jnp.int32, (nsb, sb, kd), 1)

    for h in range(HG_HEADS):
        hs = slice(h * kd, (h + 1) * kd)
        q = q_all[:, hs]
        k = kk_all[:, hs]
        v = v_all[:, hs]
        cu = cum_all[:, hs]
        s0 = st_scr[h]
        vb = v.astype(BF16)

        o_inter = _dot((q * jnp.exp(cu)).astype(BF16), s0.astype(BF16))
        blocks = [o_inter[i * sb:(i + 1) * sb] for i in range(nsb)]

        for j in range(nsb - 1):
            r0 = (j + 1) * sb
            bj = cu[r0 - 1:r0, :]
            ke = (k[j * sb:r0] * jnp.exp(bj - cu[j * sb:r0])).astype(BF16)
            qe = (q[r0:] * jnp.exp(cu[r0:] - bj)).astype(BF16)
            a = _dot_nt(qe, ke)
            lower = _dot(a.astype(BF16), vb[j * sb:r0])
            for i in range(j + 1, nsb):
                blocks[i] = blocks[i] + lower[(i - j - 1) * sb:(i - j) * sb]

        q4 = q.reshape(nsb, sb, kd)
        k4 = k.reshape(nsb, sb, kd)
        v4 = v.reshape(nsb, sb, kd)
        cu4 = cu.reshape(nsb, sb, kd)
        od = jnp.zeros((nsb, sb, kd), F32)
        for s in range(sb):
            d = cu4 - cu4[:, s:s + 1, :]
            w = jnp.where(tpos >= s, jnp.exp(jnp.minimum(d, 0.0)), 0.0)
            zz = (q4 * w * k4[:, s:s + 1, :]).reshape(c, kd).astype(BF16)
            r = _dot(zz, ones).reshape(nsb, sb, kd)
            od = od + r * v4[:, s:s + 1, :]

        o_ref[:, hs] = jnp.concatenate(blocks, axis=0) + od.reshape(c, kd)

        c_end = cu[c - 1:c, :]
        kdec = (k * jnp.exp(c_end - cu)).astype(BF16)
        upd = _dot_tn(kdec, vb)
        e_col = jnp.broadcast_to(jnp.exp(c_end), (kd, kd)).T
        st_scr[h] = e_col * s0 + upd

    @pl.when(ci == pl.num_programs(1) - 1)
    def _():
        s_ref[0] = st_scr[...]


def _hgrn_prompt(z, lb3, bsz, t):
    n = z.shape[0]
    c = _tile(t, HG_CHUNK)
    sb = min(HG_SUB, c)
    nc = t // c
    w = HG_HEADS * HG_K
    kern = functools.partial(_hgrn_prompt_kernel, c=c, sb=sb)
    return pl.pallas_call(
        kern,
        grid=(bsz, nc),
        in_specs=[
            pl.BlockSpec((c, w), lambda b, ci: (b * nc + ci, COL_HQ // w)),
            pl.BlockSpec((c, w), lambda b, ci: (b * nc + ci, COL_HF // w)),
            pl.BlockSpec((c, w), lambda b, ci: (b * nc + ci, COL_HI // w)),
            _const_spec((3, w)),
        ],
        out_specs=[
            pl.BlockSpec((c, w), lambda b, ci: (b * nc + ci, 0)),
            pl.BlockSpec((1, HG_HEADS, HG_K, HG_V), lambda b, ci: (b, 0, 0, 0)),
        ],
        out_shape=[
            jax.ShapeDtypeStruct((n, w), F32),
            jax.ShapeDtypeStruct((bsz, HG_HEADS, HG_K, HG_V), F32),
        ],
        scratch_shapes=[pltpu.VMEM((HG_HEADS, HG_K, HG_V), F32)],
        compiler_params=_cparams(("parallel", "arbitrary")),
        name="hgrn_prompt",
    )(z, z, z, lb3)


def _hgrn_sample_kernel(q_ref, f_ref, i_ref, lb_ref, s0_ref, o_ref, s_ref, *, tb):
    kd = HG_K
    logf, kk_all = _hg_gates(f_ref[...], lb_ref)
    f_all = jnp.exp(logf)
    q_all = q_ref[...]
    v_all = i_ref[...]
    for b in range(tb):
        for h in range(HG_HEADS):
            hs = slice(h * kd, (h + 1) * kd)
            e_f = jnp.broadcast_to(f_all[b:b + 1, hs], (kd, kd)).T
            e_k = jnp.broadcast_to(kk_all[b:b + 1, hs], (kd, kd)).T
            s_new = e_f * s0_ref[b, h] + e_k * v_all[b:b + 1, hs]
            s_ref[b, h] = s_new
            qrow = jnp.broadcast_to(q_all[b:b + 1, hs], (8, kd)).astype(BF16)
            o_ref[b:b + 1, hs] = _dot(qrow, s_new.astype(BF16))[0:1]


def _hgrn_sample(z, lb3, state_l):
    n = z.shape[0]
    tb = 8
    w = HG_HEADS * HG_K
    kern = functools.partial(_hgrn_sample_kernel, tb=tb)
    return pl.pallas_call(
        kern,
        grid=(n // tb,),
        in_specs=[
            pl.BlockSpec((tb, w), lambda i: (i, COL_HQ // w)),
            pl.BlockSpec((tb, w), lambda i: (i, COL_HF // w)),
            pl.BlockSpec((tb, w), lambda i: (i, COL_HI // w)),
            _const_spec((3, w)),
            pl.BlockSpec((tb, HG_HEADS, HG_K, HG_V), lambda i: (i, 0, 0, 0)),
        ],
        out_specs=[
            pl.BlockSpec((tb, w), lambda i: (i, 0)),
            pl.BlockSpec((tb, HG_HEADS, HG_K, HG_V), lambda i: (i, 0, 0, 0)),
        ],
        out_shape=[
            jax.ShapeDtypeStruct((n, w), F32),
            jax.ShapeDtypeStruct(state_l.shape, F32),
        ],
        compiler_params=_cparams(("parallel",)),
        name="hgrn_sample",
    )(z, z, z, lb3, state_l)


def _merge_kernel(att_ref, or_ref, hg_ref, ga_ref, gb_ref, x_ref, gt1_ref, sc2_ref, sh2_ref,
                  ghg_ref, g2_ref, wa_ref, wb_ref, wo_ref, x1_ref, h2_ref):
    o_r = or_ref[...]
    hg = hg_ref[...]
    ghg = ghg_ref[...]
    parts = []
    for h in range(HG_HEADS):
        hs = slice(h * HG_V, (h + 1) * HG_V)
        g = hg[:, hs]
        parts.append((_rms(o_r[:, hs], ghg) * (g * jax.nn.sigmoid(g))).astype(BF16))
    orn = jnp.concatenate(parts, axis=-1)
    oa = _dot(att_ref[...], wa_ref[...])
    ob = _dot(orn, wb_ref[...])
    m = jax.nn.sigmoid(ga_ref[...]) * oa + jax.nn.sigmoid(gb_ref[...]) * ob
    y = _dot(m.astype(BF16), wo_ref[...])
    x1 = x_ref[...] + gt1_ref[0] * y
    x1_ref[...] = x1
    h2_ref[...] = (_rms(x1, g2_ref[...]) * (1.0 + sc2_ref[0]) + sh2_ref[0]).astype(BF16)


def _merge(att, o_r, z, x, mod3, lw, st):
    n, d = x.shape
    tm, rpg, r = min(st["tm"], 256), st["rpg"], st["r"]
    w = HG_HEADS * HG_V

    def gi(i):
        return (i * tm) // rpg

    def mspec(k):
        return pl.BlockSpec((1, r, d), lambda i: (gi(i), 0, k))

    one = pl.Buffered(1)
    return pl.pallas_call(
        _merge_kernel,
        grid=(n // tm,),
        in_specs=[
            pl.BlockSpec((tm, w), lambda i: (i, 0)),
            pl.BlockSpec((tm, w), lambda i: (i, 0)),
            pl.BlockSpec((tm, w), lambda i: (i, COL_HG // w)),
            pl.BlockSpec((tm, d), lambda i: (i, COL_GA // d)),
            pl.BlockSpec((tm, d), lambda i: (i, COL_GB // d)),
            pl.BlockSpec((tm, d), lambda i: (i, 0)),
            mspec(2), mspec(4), mspec(3),
            _const_spec((1, HG_V)),
            _const_spec((1, d)),
            pl.BlockSpec((w, d), lambda i: (0, 0), pipeline_mode=one),
            pl.BlockSpec((w, d), lambda i: (0, 0), pipeline_mode=one),
            pl.BlockSpec((d, d), lambda i: (0, 0), pipeline_mode=one),
        ],
        out_specs=[
            pl.BlockSpec((tm, d), lambda i: (i, 0)),
            pl.BlockSpec((tm, d), lambda i: (i, 0)),
        ],
        out_shape=[
            jax.ShapeDtypeStruct((n, d), F32),
            jax.ShapeDtypeStruct((n, d), BF16),
        ],
        compiler_params=_cparams(("parallel",)),
        name="merge",
    )(att, o_r, z, z, z, x, mod3, mod3, mod3, lw["g_hg"], lw["g2"], lw["w_a"], lw["w_b"], lw["w_out"])


def _top_sorted(s, k):
    vals = []
    for _ in range(k):
        m = jnp.max(s, axis=0, keepdims=True)
        vals.append(m)
        s = jnp.where(s == m, -jnp.inf, s)
    return vals


def _router_kernel(h2_ref, wq_ref, k1_ref, k2_ref, s1_ref, a_ref, s2_ref, e2_ref, tau_ref, v1_scr, v2_scr):
    qp = _dot(h2_ref[...], wq_ref[...])
    half = PEER_QDIM // 2
    k1 = k1_ref[...]
    k2 = k2_ref[...]
    topk = PEER_TOPK
    for h in range(PEER_HEADS):
        qa = qp[:, h * PEER_QDIM:h * PEER_QDIM + half].astype(BF16)
        qb = qp[:, h * PEER_QDIM + half:(h + 1) * PEER_QDIM].astype(BF16)
        s1 = _dot_nt(k1, qa)
        s2 = _dot_nt(k2, qb)
        v1 = _top_sorted(s1, topk)
        v2 = _top_sorted(s2, topk)
        for a in range(topk):
            v1_scr[a:a + 1, :] = v1[a]
            v2_scr[a:a + 1, :] = v2[a]
        pieces = []
        for a in range(topk):
            nb = -(-(topk // (a + 1)) // 8) * 8
            pieces.append(v1[a] + v2_scr[0:nb, :])
        cand = jnp.concatenate(pieces, axis=0)
        sc = _top_sorted(cand, topk)
        tau = sc[topk - 1]
        m1 = v1[0]
        m2 = v2[0]
        zsum = jnp.zeros_like(tau)
        for a in range(topk):
            zsum = zsum + jnp.exp(sc[a] - sc[0])
        s1_ref[h] = s1
        tau_ref[h:h + 1, :] = tau
        a_ref[h] = jnp.exp(s1 - m1) / zsum
        s2_ref[h] = s2
        e2_ref[h] = jnp.exp(s2 - m2)


def _router(h2, lw, st):
    n, d = h2.shape
    tm = min(st["tm"], 256)
    nk = PEER_NKEYS
    half = PEER_QDIM // 2
    oshape = jax.ShapeDtypeStruct((PEER_HEADS, nk, n), F32)
    ospec = pl.BlockSpec((PEER_HEADS, nk, tm), lambda i: (0, 0, i))
    return pl.pallas_call(
        _router_kernel,
        grid=(n // tm,),
        in_specs=[
            pl.BlockSpec((tm, d), lambda i: (i, 0)),
            pl.BlockSpec((d, PEER_HEADS * PEER_QDIM), lambda i: (0, 0), pipeline_mode=pl.Buffered(1)),
            _const_spec((nk, half)),
            _const_spec((nk, half)),
        ],
        out_specs=[ospec, ospec, ospec, ospec, pl.BlockSpec((PEER_HEADS, tm), lambda i: (0, i))],
        out_shape=[oshape, oshape, oshape, oshape, jax.ShapeDtypeStruct((PEER_HEADS, n), F32)],
        scratch_shapes=[pltpu.VMEM((PEER_TOPK, tm), F32), pltpu.VMEM((PEER_TOPK, tm), F32)],
        compiler_params=_cparams(("parallel",)),
        name="peer_router",
    )(h2, lw["peer_wq"], lw["k1"], lw["k2"])


def _peer_kernel(h2_ref, u_ref, vt_ref, s1_ref, a_ref, s2_ref, e2_ref, tau_ref, x1_ref, gt2_ref, gf_ref,
                 o_ref, acc_scr, *, te, final):
    e = pl.program_id(1)
    ni = te // PEER_NKEYS
    nk = PEER_NKEYS

    @pl.when(e == 0)
    def _():
        acc_scr[...] = jnp.zeros(acc_scr.shape, F32)

    pre = _dot_nt(u_ref[...], h2_ref[...])
    act = 0.5 * pre * (1.0 + lax.erf(pre * (2.0 ** -0.5)))
    parts = []
    for ii in range(ni):
        i1 = e * ni + ii
        wsum = None
        for h in range(PEER_HEADS):
            tsum = s2_ref[h] + s1_ref[h, pl.ds(i1, 1), :]
            arow = a_ref[h, pl.ds(i1, 1), :]
            wh = jnp.where(tsum >= tau_ref[h:h + 1, :], arow * e2_ref[h], 0.0)
            wsum = wh if wsum is None else wsum + wh
        parts.append((wsum * act[ii * nk:(ii + 1) * nk]).astype(BF16))
    p = jnp.concatenate(parts, axis=0) if ni > 1 else parts[0]
    acc_scr[...] += _dot(vt_ref[...], p)

    @pl.when(e == pl.num_programs(1) - 1)
    def _():
        x2 = x1_ref[...] + gt2_ref[0] * acc_scr[...].T
        o_ref[...] = _rms(x2, gf_ref[...]) if final else x2


def _peer(h2, rt, x1, mod3, lw, g_final, st, final):
    n, d = x1.shape
    tb, rpg, r = st["tb"], st["rpg"], st["r"]
    ne = lw["u"].shape[0]
    te = 512
    nk = PEER_NKEYS

    def gi(i):
        return (i * tb) // rpg

    sspec = pl.BlockSpec((PEER_HEADS, nk, tb), lambda i, e: (0, 0, i))
    kern = functools.partial(_peer_kernel, te=te, final=final)
    return pl.pallas_call(
        kern,
        grid=(n // tb, ne // te),
        in_specs=[
            pl.BlockSpec((tb, d), lambda i, e: (i, 0)),
            pl.BlockSpec((te, d), lambda i, e: (e, 0)),
            pl.BlockSpec((d, te), lambda i, e: (0, e)),
            sspec, sspec, sspec, sspec,
            pl.BlockSpec((PEER_HEADS, tb), lambda i, e: (0, i)),
            pl.BlockSpec((tb, d), lambda i, e: (i, 0)),
            pl.BlockSpec((1, r, d), lambda i, e: (gi(i), 0, 5)),
            pl.BlockSpec((1, d), lambda i, e: (0, 0)),
        ],
        out_specs=pl.BlockSpec((tb, d), lambda i, e: (i, 0)),
        out_shape=jax.ShapeDtypeStruct((n, d), F32),
        scratch_shapes=[pltpu.VMEM((d, tb), F32)],
        compiler_params=_cparams(("parallel", "arbitrary")),
        name="peer_dense",
    )(h2, lw["u"], lw["vT"], rt[0], rt[1], rt[2], rt[3], rt[4], x1, mod3, g_final)


def _prep_layer(l, w_in, g_norm1, g_qnorm, w_uq, g_kvnorm, w_uk, w_uv, w_a, lbs, g_hg_onorm, w_b,
                w_out, g_norm2, peer_wq, peer_k1, peer_k2, peer_u, peer_v):
    d = w_in.shape[1]
    wi = w_in[l]
    o_kr = MLA_Q_RANK + MLA_KV_RANK
    o_hq = o_kr + MLA_ROPE
    hw = HG_HEADS * HG_K
    o_ga = o_hq + 4 * hw
    assert (o_hq, hw, o_ga + 2 * d, d) == (832, 1024, 9024, 2048)
    kr = wi[:, o_kr:o_hq]
    half = MLA_ROPE // 2
    kr_sw = jnp.concatenate([kr[:, half:], kr[:, :half]], axis=1)
    a_pad = jnp.zeros((d, COL_HQ - COL_A - o_hq - MLA_ROPE), F32)
    w_in_p = jnp.concatenate(
        [wi[:, o_ga:o_ga + d], wi[:, o_ga + d:o_ga + 2 * d], wi[:, :o_hq], kr_sw, a_pad, wi[:, o_hq:o_ga]],
        axis=1).astype(BF16)
    uq = w_uq[l].reshape(MLA_Q_RANK, MLA_HEADS, MLA_NOPE + MLA_ROPE)
    w_nope = uq[:, :, :MLA_NOPE].reshape(MLA_Q_RANK, MLA_HEADS * MLA_NOPE)
    rope = uq[:, :, MLA_NOPE:]
    rope_sw = jnp.concatenate([rope[..., half:], rope[..., :half]], axis=-1)
    lb = lbs[l].reshape(1, hw)
    lb3 = jnp.concatenate([jnp.log(lb), jnp.log1p(-lb), 1.0 - lb], axis=0)
    return {
        "w_in_p": w_in_p,
        "g1": g_norm1[l].reshape(1, d),
        "g_q": g_qnorm[l].reshape(1, MLA_Q_RANK),
        "g_kv": g_kvnorm[l].reshape(1, MLA_KV_RANK),
        "w_nope": w_nope.astype(BF16),
        "w_rope": rope.reshape(MLA_Q_RANK, MLA_HEADS * MLA_ROPE).astype(BF16),
        "w_rope_sw": rope_sw.reshape(MLA_Q_RANK, MLA_HEADS * MLA_ROPE).astype(BF16),
        "w_ukT": jnp.transpose(w_uk[l], (1, 2, 0)).astype(BF16),
        "w_uv": jnp.transpose(w_uv[l], (1, 0, 2)).astype(BF16),
        "w_a": w_a[l].astype(BF16),
        "w_b": w_b[l].astype(BF16),
        "w_out": w_out[l].astype(BF16),
        "lb3": lb3,
        "g_hg": g_hg_onorm[l].reshape(1, HG_V),
        "g2": g_norm2[l].reshape(1, d),
        "peer_wq": peer_wq[l].astype(BF16),
        "k1": peer_k1[l].astype(BF16),
        "k2": peer_k2[l].astype(BF16),
        "u": peer_u[l].astype(BF16),
        "vT": peer_v[l].T.astype(BF16),
    }


def _rope_tables(pos):
    half = MLA_ROPE // 2
    freq = ROPE_THETA ** (-jnp.arange(half, dtype=F32) / half)
    ang = pos.astype(F32)[:, None] * freq[None, :]
    cos = jnp.cos(ang)
    sin = jnp.sin(ang)
    cos_t = jnp.tile(jnp.concatenate([cos, cos], axis=1), (1, MLA_HEADS))
    sin_t = jnp.tile(jnp.concatenate([-sin, sin], axis=1), (1, MLA_HEADS))
    return cos_t, sin_t


def kernel(x_prompt, x_sample, cache_ckv, cache_krope, state_hgrn, page_table, c_prompt, c_sample, w_ada, b_ada, g_norm1, w_in, g_qnorm, w_uq, g_kvnorm, w_uk, w_uv, w_a, hg_lb_logits, g_hg_onorm, w_b, w_out, g_norm2, peer_wq, peer_k1, peer_k2, peer_u, peer_v, g_final):
    bsz, t, d = x_prompt.shape
    nsq, ts, _ = x_sample.shape
    assert ts == 1
    depth = w_in.shape[0]
    n_p = bsz * t
    n_s = nsq

    lbs = jnp.cumsum(jax.nn.softmax(hg_lb_logits.astype(F32), axis=0), axis=0)
    lbs = lbs - lbs[0]
    layers = [
        _prep_layer(l, w_in, g_norm1, g_qnorm, w_uq, g_kvnorm, w_uk, w_uv, w_a, lbs, g_hg_onorm, w_b,
                    w_out, g_norm2, peer_wq, peer_k1, peer_k2, peer_u, peer_v)
        for l in range(depth)
    ]
    gf = g_final.reshape(1, d)

    mod = _ada_mod(jnp.concatenate([c_prompt, c_sample], axis=0), w_ada, b_ada)
    cos_p, sin_p = _rope_tables(jnp.arange(t))
    past_len = page_table.shape[1] * cache_ckv.shape[2]
    cos_s, sin_s = _rope_tables(past_len + jnp.arange(1))

    st_p = {"tm": _tile(t, 512), "tb": _tile(t, 512), "rpg": t, "r": 1}
    st_s = {"tm": n_s, "tb": n_s, "rpg": n_s, "r": n_s}

    xp = x_prompt.reshape(n_p, d)
    xs = x_sample.reshape(n_s, d)
    ckv_p, kr_p, stt_p, ckv_s, kr_s, stt_s = [], [], [], [], [], []
    for l in range(depth):
        lw = layers[l]
        final = l == depth - 1
        mod_p = mod[l, :bsz].reshape(bsz, 1, 6 * d)
        mod_s = mod[l, bsz:].reshape(1, n_s, 6 * d)

        z = _in_proj(xp, mod_p, lw["g1"], lw["w_in_p"], st_p)
        q, ckv, kr, kvb = _mla_prep(z, lw, cos_p, sin_p, st_p)
        att = _attn_prompt(q, kvb, lw["w_uv"], bsz, t)
        o_r, s_end = _hgrn_prompt(z, lw["lb3"], bsz, t)
        x1, h2 = _merge(att, o_r, z, xp, mod_p, lw, st_p)
        rt = _router(h2, lw, st_p)
        xp = _peer(h2, rt, x1, mod_p, lw, gf, st_p, final)
        ckv_p.append(ckv.reshape(bsz, t, MLA_KV_RANK))
        kr_p.append(kr.reshape(bsz, t, MLA_ROPE))
        stt_p.append(s_end)

        z = _in_proj(xs, mod_s, lw["g1"], lw["w_in_p"], st_s)
        q, ckv, kr, kvb = _mla_prep(z, lw, cos_s, sin_s, st_s)
        lat = _attn_sample(jnp.transpose(q, (1, 0, 2)), kvb, cache_ckv, cache_krope, page_table, l)
        att = _uv_proj(jnp.transpose(lat, (1, 0, 2)), lw["w_uv"])
        o_r, s_new = _hgrn_sample(z, lw["lb3"], state_hgrn[l])
        x1, h2 = _merge(att, o_r, z, xs, mod_s, lw, st_s)
        rt = _router(h2, lw, st_s)
        xs = _peer(h2, rt, x1, mod_s, lw, gf, st_s, final)
        ckv_s.append(ckv.reshape(n_s, 1, MLA_KV_RANK))
        kr_s.append(kr.reshape(n_s, 1, MLA_ROPE))
        stt_s.append(s_new)

    return (xp.reshape(bsz, t, d), xs.reshape(n_s, 1, d),
            jnp.stack(ckv_p), jnp.stack(kr_p), jnp.stack(stt_p),
            jnp.stack(ckv_s), jnp.stack(kr_s), jnp.stack(stt_s))
```

```python
import functools

import jax
import jax.numpy as jnp
from jax import lax
from jax.experimental import pallas as pl
from jax.experimental.pallas import tpu as pltpu

F32 = jnp.float32
BF16 = jnp.bfloat16

MLA_HEADS = 8
MLA_Q_RANK = 512
MLA_KV_RANK = 256
MLA_NOPE = 128
MLA_ROPE = 64
MLA_V = 128
MLA_SCALE = (MLA_NOPE + MLA_ROPE) ** -0.5
ROPE_THETA = 10000.0
HG_HEADS = 8
HG_K = 128
HG_V = 128
PEER_HEADS = 8
PEER_NKEYS = 128
PEER_QDIM = 256
PEER_TOPK = 16
EPS = 1e-6

QK_PAD = 384
HG_CHUNK = 64
HG_SUB = 16
VMEM_LIMIT = 56 * 1024 * 1024
ATT_TQ, ATT_TK, ATT_GP = 256, 512, 32
LANE, SUBLANE = 128, 8
PEER_ROWS = 32
PEER_MM_LANES = 256

COL_GA, COL_GB, COL_A, COL_HQ, COL_HF, COL_HI, COL_HG, IN_PAD = 0, 2048, 4096, 5120, 6144, 7168, 8192, 9216


def _cparams(sem, flags=None):
    return pltpu.CompilerParams(dimension_semantics=sem, vmem_limit_bytes=VMEM_LIMIT, flags=flags)


def _tile(n, pref):
    t = min(n, pref)
    while n % t:
        t -= 8
    return t


def _dot(a, b):
    return jnp.dot(a, b, preferred_element_type=F32)


def _dot_nt(a, b):
    return lax.dot_general(a, b, (((1,), (1,)), ((), ())), preferred_element_type=F32)


def _dot_tn(a, b):
    return lax.dot_general(a, b, (((0,), (0,)), ((), ())), preferred_element_type=F32)


def _split3(x):
    hi = x.astype(BF16)
    r = x - hi.astype(F32)
    mid = r.astype(BF16)
    lo = (r - mid.astype(F32)).astype(BF16)
    return hi, mid, lo


def _rms(x, g):
    return x * lax.rsqrt(jnp.mean(x * x, axis=-1, keepdims=True) + EPS) * g


def _const_spec(shape):
    nd = len(shape)
    return pl.BlockSpec(shape, lambda *_: (0,) * nd)


def _ada_kernel(c_ref, w_ref, b_ref, o_ref):
    c = c_ref[...]
    cs = c * jax.nn.sigmoid(c)
    a_hi, a_mid, _ = _split3(cs)
    w_hi, w_mid, _ = _split3(w_ref[0])
    o_ref[0] = _dot(a_hi, w_hi) + _dot(a_mid, w_hi) + _dot(a_hi, w_mid) + b_ref[0]


def _ada_mod(c_all, w_ada, b_ada):
    depth, d, w6 = w_ada.shape
    r = c_all.shape[0]
    tn = 1024
    return pl.pallas_call(
        _ada_kernel,
        grid=(depth, w6 // tn),
        in_specs=[
            pl.BlockSpec((r, d), lambda l, j: (0, 0)),
            pl.BlockSpec((1, d, tn), lambda l, j: (l, 0, j)),
            pl.BlockSpec((1, 1, tn), lambda l, j: (l, 0, j)),
        ],
        out_specs=pl.BlockSpec((1, r, tn), lambda l, j: (l, 0, j)),
        out_shape=jax.ShapeDtypeStruct((depth, r, w6), F32),
        compiler_params=_cparams(("parallel", "parallel")),
        name="ada_mod",
    )(c_all, w_ada, b_ada.reshape(depth, 1, w6))


def _in_proj_kernel(x_ref, sc_ref, sh_ref, g_ref, w_ref, o_ref, h_scr):
    @pl.when(pl.program_id(1) == 0)
    def _():
        h = _rms(x_ref[...], g_ref[...]) * (1.0 + sc_ref[0]) + sh_ref[0]
        h_scr[...] = h.astype(BF16)

    o_ref[...] = _dot(h_scr[...], w_ref[...])


def _in_proj(x, mod3, g1, w_in_p, st):
    n, d = x.shape
    tm, rpg, r = st["tm"], st["rpg"], st["r"]
    tn = 1024
    width = w_in_p.shape[1]

    def gi(i):
        return (i * tm) // rpg

    return pl.pallas_call(
        _in_proj_kernel,
        grid=(n // tm, width // tn),
        in_specs=[
            pl.BlockSpec((tm, d), lambda i, j: (i, 0)),
            pl.BlockSpec((1, r, d), lambda i, j: (gi(i), 0, 1)),
            pl.BlockSpec((1, r, d), lambda i, j: (gi(i), 0, 0)),
            pl.BlockSpec((1, d), lambda i, j: (0, 0)),
            pl.BlockSpec((d, tn), lambda i, j: (0, j)),
        ],
        out_specs=pl.BlockSpec((tm, tn), lambda i, j: (i, j)),
        out_shape=jax.ShapeDtypeStruct((n, width), F32),
        scratch_shapes=[pltpu.VMEM((tm, d), BF16)],
        compiler_params=_cparams(("parallel", "arbitrary")),
        name="in_proj",
    )(x, mod3, mod3, g1, w_in_p)


def _mla_prep_kernel(z_ref, gq_ref, gkv_ref, wn_ref, wr_ref, wrs_ref, wuk_ref, cos_ref, sin_ref,
                     q_ref, ckv_ref, kr_ref, kvb_ref):
    z = z_ref[...]
    cq = z[:, :MLA_Q_RANK]
    ckv = z[:, MLA_Q_RANK:MLA_Q_RANK + MLA_KV_RANK]
    o = MLA_Q_RANK + MLA_KV_RANK
    kr = z[:, o:o + MLA_ROPE]
    krs = z[:, o + MLA_ROPE:o + 2 * MLA_ROPE]
    cos = cos_ref[...]
    sin = sin_ref[...]

    cqn = _rms(cq, gq_ref[...]).astype(BF16)
    qn = _dot(cqn, wn_ref[...])
    qrope = (_dot(cqn, wr_ref[...]) * cos + _dot(cqn, wrs_ref[...]) * sin) * MLA_SCALE
    tm = z.shape[0]
    zpad = jnp.zeros((tm, QK_PAD - MLA_KV_RANK - MLA_ROPE), BF16)
    for h in range(MLA_HEADS):
        ql = _dot(qn[:, h * MLA_NOPE:(h + 1) * MLA_NOPE].astype(BF16), wuk_ref[h]) * MLA_SCALE
        q_ref[h, :, 0:MLA_KV_RANK] = ql.astype(BF16)
        q_ref[h, :, MLA_KV_RANK:MLA_KV_RANK + MLA_ROPE] = qrope[:, h * MLA_ROPE:(h + 1) * MLA_ROPE].astype(BF16)
        q_ref[h, :, MLA_KV_RANK + MLA_ROPE:QK_PAD] = zpad

    ckvn = _rms(ckv, gkv_ref[...])
    kro = kr * cos[:, :MLA_ROPE] + krs * sin[:, :MLA_ROPE]
    ckv_ref[...] = ckvn
    kr_ref[...] = kro
    kvb_ref[:, 0:MLA_KV_RANK] = ckvn.astype(BF16)
    kvb_ref[:, MLA_KV_RANK:MLA_KV_RANK + MLA_ROPE] = kro.astype(BF16)
    kvb_ref[:, MLA_KV_RANK + MLA_ROPE:QK_PAD] = zpad


def _mla_prep(z, lw, cos_t, sin_t, st):
    n = z.shape[0]
    tm = st["tm"]
    rt = cos_t.shape[0]
    hr = MLA_HEADS * MLA_ROPE
    if rt == 1:
        tspec = pl.BlockSpec((1, hr), lambda i: (0, 0))
    else:
        nt = rt // tm
        tspec = pl.BlockSpec((tm, hr), lambda i: (i % nt, 0))
    return pl.pallas_call(
        _mla_prep_kernel,
        grid=(n // tm,),
        in_specs=[
            pl.BlockSpec((tm, 1024), lambda i: (i, COL_A // 1024)),
            _const_spec((1, MLA_Q_RANK)),
            _const_spec((1, MLA_KV_RANK)),
            _const_spec((MLA_Q_RANK, MLA_HEADS * MLA_NOPE)),
            _const_spec((MLA_Q_RANK, hr)),
            _const_spec((MLA_Q_RANK, hr)),
            _const_spec((MLA_HEADS, MLA_NOPE, MLA_KV_RANK)),
            tspec,
            tspec,
        ],
        out_specs=[
            pl.BlockSpec((MLA_HEADS, tm, QK_PAD), lambda i: (0, i, 0)),
            pl.BlockSpec((tm, MLA_KV_RANK), lambda i: (i, 0)),
            pl.BlockSpec((tm, MLA_ROPE), lambda i: (i, 0)),
            pl.BlockSpec((tm, QK_PAD), lambda i: (i, 0)),
        ],
        out_shape=[
            jax.ShapeDtypeStruct((MLA_HEADS, n, QK_PAD), BF16),
            jax.ShapeDtypeStruct((n, MLA_KV_RANK), F32),
            jax.ShapeDtypeStruct((n, MLA_ROPE), F32),
            jax.ShapeDtypeStruct((n, QK_PAD), BF16),
        ],
        compiler_params=_cparams(("parallel",)),
        name="mla_prep",
    )(z, lw["g_q"], lw["g_kv"], lw["w_nope"], lw["w_rope"], lw["w_rope_sw"], lw["w_ukT"], cos_t, sin_t)


def _attn_prompt_kernel(q_ref, k_ref, wuv_ref, o_ref, m_scr, l_scr, acc_scr, *, tq, tk):
    i = pl.program_id(1)
    j = pl.program_id(2)
    nh = MLA_HEADS

    @pl.when(j == 0)
    def _():
        m_scr[...] = jnp.full(m_scr.shape, -jnp.inf, F32)
        l_scr[...] = jnp.zeros(l_scr.shape, F32)
        acc_scr[...] = jnp.zeros(acc_scr.shape, F32)

    def block(masked):
        k = k_ref[...]
        v = k[:, :MLA_KV_RANK]
        if masked:
            qpos = i * tq + lax.broadcasted_iota(jnp.int32, (tq, tk), 0)
            kpos = j * tk + lax.broadcasted_iota(jnp.int32, (tq, tk), 1)
            keep = kpos <= qpos
        for h in range(nh):
            s = _dot_nt(q_ref[h], k)
            if masked:
                s = jnp.where(keep, s, -jnp.inf)
            m_prev = m_scr[h]
            m_new = jnp.maximum(m_prev, jnp.max(s, axis=-1, keepdims=True))
            corr = jnp.exp(m_prev - m_new)
            p = jnp.exp(s - m_new)
            l_scr[h] = l_scr[h] * corr + jnp.sum(p, axis=-1, keepdims=True)
            acc_scr[h] = acc_scr[h] * corr + _dot(p.astype(BF16), v)
            m_scr[h] = m_new

    @pl.when(j * tk + tk - 1 <= i * tq)
    def _():
        block(False)

    @pl.when(jnp.logical_and(j * tk + tk - 1 > i * tq, j * tk <= i * tq + tq - 1))
    def _():
        block(True)

    @pl.when(j == pl.num_programs(2) - 1)
    def _():
        for h in range(nh):
            lat = acc_scr[h] / l_scr[h]
            o_ref[:, h * MLA_V:(h + 1) * MLA_V] = _dot(lat.astype(BF16), wuv_ref[h]).astype(BF16)


def _attn_prompt(q, kvb, wuv, bsz, t):
    n = kvb.shape[0]
    tq = _tile(t, ATT_TQ)
    tk = _tile(t, ATT_TK)
    nq, nk = t // tq, t // tk
    kern = functools.partial(_attn_prompt_kernel, tq=tq, tk=tk)
    return pl.pallas_call(
        kern,
        grid=(bsz, nq, nk),
        in_specs=[
            pl.BlockSpec((MLA_HEADS, tq, QK_PAD), lambda b, i, j: (0, b * nq + i, 0)),
            pl.BlockSpec((tk, QK_PAD), lambda b, i, j: (b * nk + jnp.minimum(j, (i * tq + tq - 1) // tk), 0)),
            _const_spec((MLA_HEADS, MLA_KV_RANK, MLA_V)),
        ],
        out_specs=pl.BlockSpec((tq, MLA_HEADS * MLA_V), lambda b, i, j: (b * nq + i, 0)),
        out_shape=jax.ShapeDtypeStruct((n, MLA_HEADS * MLA_V), BF16),
        scratch_shapes=[
            pltpu.VMEM((MLA_HEADS, tq, 1), F32),
            pltpu.VMEM((MLA_HEADS, tq, 1), F32),
            pltpu.VMEM((MLA_HEADS, tq, MLA_KV_RANK), F32),
        ],
        compiler_params=_cparams(("parallel", "parallel", "arbitrary")),
        name="attn_prompt",
    )(q, kvb, wuv)


def _attn_sample_kernel(pt_ref, q_ref, knew_ref, ckv_hbm, kr_hbm, o_ref,
                        kbuf, rbuf, sem, m_scr, l_scr, acc_scr, *, layer, gp, ng, npg, nsteps):
    b = pl.program_id(0)
    g = pl.program_id(1)
    step = b * ng + g
    slot = step % 2

    def copies(st, sl):
        base = (st // ng) * npg + (st % ng) * gp
        cps = []
        for k in range(gp):
            page = pt_ref[base + k]
            cps.append(pltpu.make_async_copy(ckv_hbm.at[layer, page], kbuf.at[sl, k], sem.at[0, sl]))
            cps.append(pltpu.make_async_copy(kr_hbm.at[layer, page], rbuf.at[sl, k], sem.at[1, sl]))
        return cps

    @pl.when(step == 0)
    def _():
        for c in copies(step, slot):
            c.start()

    @pl.when(step + 1 < nsteps)
    def _():
        for c in copies(step + 1, 1 - slot):
            c.start()

    for c in copies(step, slot):
        c.wait()

    @pl.when(g == 0)
    def _():
        m_scr[...] = jnp.full(m_scr.shape, -jnp.inf, F32)
        l_scr[...] = jnp.zeros(l_scr.shape, F32)
        acc_scr[...] = jnp.zeros(acc_scr.shape, F32)

    q = q_ref[0]
    ps = kbuf.shape[2]
    kc = kbuf[slot].reshape(gp * ps, MLA_KV_RANK).astype(BF16)
    qr = q[:, MLA_KV_RANK:MLA_KV_RANK + MLA_ROPE]
    s_rope = jnp.concatenate([_dot(qr, rbuf[slot, k].astype(BF16)) for k in range(gp)], axis=1)
    s = _dot_nt(q[:, :MLA_KV_RANK], kc) + s_rope
    m_prev = m_scr[...]
    m_new = jnp.maximum(m_prev, jnp.max(s, axis=-1, keepdims=True))
    corr = jnp.exp(m_prev - m_new)
    p = jnp.exp(s - m_new)
    l_scr[...] = l_scr[...] * corr + jnp.sum(p, axis=-1, keepdims=True)
    acc_scr[...] = acc_scr[...] * corr + _dot(p.astype(BF16), kc)
    m_scr[...] = m_new

    @pl.when(g == ng - 1)
    def _():
        knew = knew_ref[0].astype(F32)
        s_new = jnp.sum(q.astype(F32) * knew, axis=-1, keepdims=True)
        m_old = m_scr[...]
        m_fin = jnp.maximum(m_old, s_new)
        c_old = jnp.exp(m_old - m_fin)
        p_new = jnp.exp(s_new - m_fin)
        l_fin = l_scr[...] * c_old + p_new
        acc = acc_scr[...] * c_old + p_new * knew[:, :MLA_KV_RANK]
        o_ref[0] = acc / l_fin


def _attn_sample(q_nh, kvb, cache_ckv, cache_krope_t, page_table, layer):
    n = q_nh.shape[0]
    npg = page_table.shape[1]
    ps = cache_ckv.shape[2]
    gp = min(npg, ATT_GP)
    while npg % gp:
        gp -= 1
    ng = npg // gp
    kern = functools.partial(_attn_sample_kernel, layer=layer, gp=gp, ng=ng, npg=npg, nsteps=n * ng)
    grid_spec = pltpu.PrefetchScalarGridSpec(
        num_scalar_prefetch=1,
        grid=(n, ng),
        in_specs=[
            pl.BlockSpec((1, MLA_HEADS, QK_PAD), lambda b, g, pt: (b, 0, 0)),
            pl.BlockSpec((1, 1, QK_PAD), lambda b, g, pt: (b, 0, 0)),
            pl.BlockSpec(memory_space=pl.ANY),
            pl.BlockSpec(memory_space=pl.ANY),
        ],
        out_specs=pl.BlockSpec((1, MLA_HEADS, MLA_KV_RANK), lambda b, g, pt: (b, 0, 0)),
        scratch_shapes=[
            pltpu.VMEM((2, gp, ps, MLA_KV_RANK), F32),
            pltpu.VMEM((2, gp, MLA_ROPE, ps), F32),
            pltpu.SemaphoreType.DMA((2, 2)),
            pltpu.VMEM((MLA_HEADS, 1), F32),
            pltpu.VMEM((MLA_HEADS, 1), F32),
            pltpu.VMEM((MLA_HEADS, MLA_KV_RANK), F32),
        ],
    )
    return pl.pallas_call(
        kern,
        grid_spec=grid_spec,
        out_shape=jax.ShapeDtypeStruct((n, MLA_HEADS, MLA_KV_RANK), F32),
        compiler_params=_cparams(("arbitrary", "arbitrary")),
        name="attn_sample",
    )(page_table.reshape(-1), q_nh, kvb.reshape(n, 1, QK_PAD), cache_ckv, cache_krope_t)


def _uv_proj_kernel(lat_ref, wuv_ref, o_ref):
    for h in range(MLA_HEADS):
        o_ref[:, h * MLA_V:(h + 1) * MLA_V] = _dot(lat_ref[h].astype(BF16), wuv_ref[h]).astype(BF16)


def _uv_proj(lat_hn, wuv):
    n = lat_hn.shape[1]
    return pl.pallas_call(
        _uv_proj_kernel,
        grid=(1,),
        in_specs=[_const_spec(lat_hn.shape), _const_spec(wuv.shape)],
        out_specs=_const_spec((n, MLA_HEADS * MLA_V)),
        out_shape=jax.ShapeDtypeStruct((n, MLA_HEADS * MLA_V), BF16),
        compiler_params=_cparams(("arbitrary",)),
        name="uv_proj",
    )(lat_hn, wuv)


def _hg_gates(fp, lb_ref):
    lsig = jnp.minimum(fp, 0.0) - jnp.log1p(jnp.exp(-jnp.abs(fp)))
    bterm = lb_ref[1:2, :] + lsig
    loglb = lb_ref[0:1, :]
    logf = jnp.maximum(loglb, bterm) + jnp.log1p(jnp.exp(-jnp.abs(loglb - bterm)))
    kk = lb_ref[2:3, :] * jax.nn.sigmoid(-fp)
    return logf, kk


def _hgrn_prompt_kernel(q_ref, f_ref, i_ref, lb_ref, o_ref, s_ref, st_scr, *, c, sb):
    ci = pl.program_id(1)
    nsb = c // sb
    kd = HG_K

    @pl.when(ci == 0)
    def _():
        st_scr[...] = jnp.zeros(st_scr.shape, F32)

    logf, kk_all = _hg_gates(f_ref[...], lb_ref)
    tril = (lax.broadcasted_iota(jnp.int32, (c, c), 0) >= lax.broadcasted_iota(jnp.int32, (c, c), 1)).astype(BF16)
    l_hi, l_mid, l_lo = _split3(logf)
    cum_all = _dot(tril, l_hi) + _dot(tril, l_mid) + _dot(tril, l_lo)
    q_all = q_ref[...]
    v_all = i_ref[...]
    ones = jnp.ones((kd, kd), BF16)
    tpos = lax.broadcasted_iota(jnp.int32, (nsb, sb, kd), 1)

    for h in range(HG_HEADS):
        hs = slice(h * kd, (h + 1) * kd)
        q = q_all[:, hs]
        k = kk_all[:, hs]
        v = v_all[:, hs]
        cu = cum_all[:, hs]
        s0 = st_scr[h]
        vb = v.astype(BF16)

        o_inter = _dot((q * jnp.exp(cu)).astype(BF16), s0.astype(BF16))
        blocks = [o_inter[i * sb:(i + 1) * sb] for i in range(nsb)]

        for j in range(nsb - 1):
            r0 = (j + 1) * sb
            bj = cu[r0 - 1:r0, :]
            ke = (k[j * sb:r0] * jnp.exp(bj - cu[j * sb:r0])).astype(BF16)
            qe = (q[r0:] * jnp.exp(cu[r0:] - bj)).astype(BF16)
            a = _dot_nt(qe, ke)
            lower = _dot(a.astype(BF16), vb[j * sb:r0])
            for i in range(j + 1, nsb):
                blocks[i] = blocks[i] + lower[(i - j - 1) * sb:(i - j) * sb]

        q4 = q.reshape(nsb, sb, kd)
        k4 = k.reshape(nsb, sb, kd)
        v4 = v.reshape(nsb, sb, kd)
        cu4 = cu.reshape(nsb, sb, kd)
        od = jnp.zeros((nsb, sb, kd), F32)
        for s in range(sb):
            d = cu4 - cu4[:, s:s + 1, :]
            w = jnp.where(tpos >= s, jnp.exp(jnp.minimum(d, 0.0)), 0.0)
            zz = (q4 * w * k4[:, s:s + 1, :]).reshape(c, kd).astype(BF16)
            r = _dot(zz, ones).reshape(nsb, sb, kd)
            od = od + r * v4[:, s:s + 1, :]

        o_ref[:, hs] = jnp.concatenate(blocks, axis=0) + od.reshape(c, kd)

        c_end = cu[c - 1:c, :]
        kdec = (k * jnp.exp(c_end - cu)).astype(BF16)
        upd = _dot_tn(kdec, vb)
        e_col = jnp.broadcast_to(jnp.exp(c_end), (kd, kd)).T
        st_scr[h] = e_col * s0 + upd

    @pl.when(ci == pl.num_programs(1) - 1)
    def _():
        s_ref[0] = st_scr[...]


def _hgrn_prompt(z, lb3, bsz, t):
    n = z.shape[0]
    c = _tile(t, HG_CHUNK)
    sb = min(HG_SUB, c)
    nc = t // c
    w = HG_HEADS * HG_K
    kern = functools.partial(_hgrn_prompt_kernel, c=c, sb=sb)
    return pl.pallas_call(
        kern,
        grid=(bsz, nc),
        in_specs=[
            pl.BlockSpec((c, w), lambda b, ci: (b * nc + ci, COL_HQ // w)),
            pl.BlockSpec((c, w), lambda b, ci: (b * nc + ci, COL_HF // w)),
            pl.BlockSpec((c, w), lambda b, ci: (b * nc + ci, COL_HI // w)),
            _const_spec((3, w)),
        ],
        out_specs=[
            pl.BlockSpec((c, w), lambda b, ci: (b * nc + ci, 0)),
            pl.BlockSpec((1, HG_HEADS, HG_K, HG_V), lambda b, ci: (b, 0, 0, 0)),
        ],
        out_shape=[
            jax.ShapeDtypeStruct((n, w), F32),
            jax.ShapeDtypeStruct((bsz, HG_HEADS, HG_K, HG_V), F32),
        ],
        scratch_shapes=[pltpu.VMEM((HG_HEADS, HG_K, HG_V), F32)],
        compiler_params=_cparams(("parallel", "arbitrary")),
        name="hgrn_prompt",
    )(z, z, z, lb3)


def _hgrn_sample_kernel(q_ref, f_ref, i_ref, lb_ref, s0_ref, o_ref, s_ref, *, tb):
    kd = HG_K
    logf, kk_all = _hg_gates(f_ref[...], lb_ref)
    f_all = jnp.exp(logf)
    q_all = q_ref[...]
    v_all = i_ref[...]
    for b in range(tb):
        for h in range(HG_HEADS):
            hs = slice(h * kd, (h + 1) * kd)
            e_f = jnp.broadcast_to(f_all[b:b + 1, hs], (kd, kd)).T
            e_k = jnp.broadcast_to(kk_all[b:b + 1, hs], (kd, kd)).T
            s_new = e_f * s0_ref[b, h] + e_k * v_all[b:b + 1, hs]
            s_ref[b, h] = s_new
            qrow = jnp.broadcast_to(q_all[b:b + 1, hs], (8, kd)).astype(BF16)
            o_ref[b:b + 1, hs] = _dot(qrow, s_new.astype(BF16))[0:1]


def _hgrn_sample(z, lb3, state_l):
    n = z.shape[0]
    tb = 8
    w = HG_HEADS * HG_K
    kern = functools.partial(_hgrn_sample_kernel, tb=tb)
    return pl.pallas_call(
        kern,
        grid=(n // tb,),
        in_specs=[
            pl.BlockSpec((tb, w), lambda i: (i, COL_HQ // w)),
            pl.BlockSpec((tb, w), lambda i: (i, COL_HF // w)),
            pl.BlockSpec((tb, w), lambda i: (i, COL_HI // w)),
            _const_spec((3, w)),
            pl.BlockSpec((tb, HG_HEADS, HG_K, HG_V), lambda i: (i, 0, 0, 0)),
        ],
        out_specs=[
            pl.BlockSpec((tb, w), lambda i: (i, 0)),
            pl.BlockSpec((tb, HG_HEADS, HG_K, HG_V), lambda i: (i, 0, 0, 0)),
        ],
        out_shape=[
            jax.ShapeDtypeStruct((n, w), F32),
            jax.ShapeDtypeStruct(state_l.shape, F32),
        ],
        compiler_params=_cparams(("parallel",)),
        name="hgrn_sample",
    )(z, z, z, lb3, state_l)


def _merge_kernel(att_ref, or_ref, hg_ref, ga_ref, gb_ref, x_ref, gt1_ref, sc2_ref, sh2_ref,
                  ghg_ref, g2_ref, wa_ref, wb_ref, wo_ref, x1_ref, h2_ref, h2t_ref):
    o_r = or_ref[...]
    hg = hg_ref[...]
    ghg = ghg_ref[...]
    parts = []
    for h in range(HG_HEADS):
        hs = slice(h * HG_V, (h + 1) * HG_V)
        g = hg[:, hs]
        parts.append((_rms(o_r[:, hs], ghg) * (g * jax.nn.sigmoid(g))).astype(BF16))
    orn = jnp.concatenate(parts, axis=-1)
    oa = _dot(att_ref[...], wa_ref[...])
    ob = _dot(orn, wb_ref[...])
    m = jax.nn.sigmoid(ga_ref[...]) * oa + jax.nn.sigmoid(gb_ref[...]) * ob
    y = _dot(m.astype(BF16), wo_ref[...])
    x1 = x_ref[...] + gt1_ref[0] * y
    x1_ref[...] = x1
    h2 = _rms(x1, g2_ref[...]) * (1.0 + sc2_ref[0]) + sh2_ref[0]
    h2_ref[...] = h2.astype(BF16)
    h2t_ref[...] = h2.T.astype(BF16)


def _merge(att, o_r, z, x, mod3, lw, st):
    n, d = x.shape
    tm, rpg, r = min(st["tm"], 256), st["rpg"], st["r"]
    w = HG_HEADS * HG_V

    def gi(i):
        return (i * tm) // rpg

    def mspec(k):
        return pl.BlockSpec((1, r, d), lambda i: (gi(i), 0, k))

    one = pl.Buffered(1)
    return pl.pallas_call(
        _merge_kernel,
        grid=(n // tm,),
        in_specs=[
            pl.BlockSpec((tm, w), lambda i: (i, 0)),
            pl.BlockSpec((tm, w), lambda i: (i, 0)),
            pl.BlockSpec((tm, w), lambda i: (i, COL_HG // w)),
            pl.BlockSpec((tm, d), lambda i: (i, COL_GA // d)),
            pl.BlockSpec((tm, d), lambda i: (i, COL_GB // d)),
            pl.BlockSpec((tm, d), lambda i: (i, 0)),
            mspec(2), mspec(4), mspec(3),
            _const_spec((1, HG_V)),
            _const_spec((1, d)),
            pl.BlockSpec((w, d), lambda i: (0, 0), pipeline_mode=one),
            pl.BlockSpec((w, d), lambda i: (0, 0), pipeline_mode=one),
            pl.BlockSpec((d, d), lambda i: (0, 0), pipeline_mode=one),
        ],
        out_specs=[
            pl.BlockSpec((tm, d), lambda i: (i, 0)),
            pl.BlockSpec((tm, d), lambda i: (i, 0)),
            pl.BlockSpec((d, tm), lambda i: (0, i)),
        ],
        out_shape=[
            jax.ShapeDtypeStruct((n, d), F32),
            jax.ShapeDtypeStruct((n, d), BF16),
            jax.ShapeDtypeStruct((d, n), BF16),
        ],
        compiler_params=_cparams(("parallel",)),
        name="merge",
    )(att, o_r, z, z, z, x, mod3, mod3, mod3, lw["g_hg"], lw["g2"], lw["w_a"], lw["w_b"], lw["w_out"])


def _top_sorted(s, k):
    vals = []
    for _ in range(k):
        m = jnp.max(s, axis=0, keepdims=True)
        vals.append(m)
        s = jnp.where(s == m, -jnp.inf, s)
    return vals


def _router_kernel(h2_ref, wq_ref, k1_ref, k2_ref, thr_ref, a_ref, s2_ref, e2_ref, v2_scr):
    qp = _dot(h2_ref[...], wq_ref[...])
    half = PEER_QDIM // 2
    k1 = k1_ref[...]
    k2 = k2_ref[...]
    topk = PEER_TOPK
    nx = topk + 1
    for h in range(PEER_HEADS):
        qa = qp[:, h * PEER_QDIM:h * PEER_QDIM + half].astype(BF16)
        qb = qp[:, h * PEER_QDIM + half:(h + 1) * PEER_QDIM].astype(BF16)
        s1 = _dot_nt(k1, qa)
        s2 = _dot_nt(k2, qb)
        v1 = _top_sorted(s1, nx)
        v2 = _top_sorted(s2, nx)
        v2_scr[...] = jnp.full(v2_scr.shape, -jnp.inf, F32)
        for a in range(nx):
            v2_scr[a:a + 1, :] = v2[a]
        pieces = []
        for a in range(nx):
            nb = -(-(nx // (a + 1)) // 8) * 8
            pieces.append(v1[a] + v2_scr[0:nb, :])
        cand = jnp.concatenate(pieces, axis=0)
        sc = _top_sorted(cand, nx)
        mid = 0.5 * (sc[topk - 1] + sc[topk])
        zsum = jnp.zeros_like(mid)
        for a in range(topk):
            zsum = zsum + jnp.exp(sc[a] - sc[0])
        thr = mid - s1
        aw = jnp.exp(s1 - v1[0]) / zsum
        e2 = jnp.exp(s2 - v2[0])
        for j in range(thr_ref.shape[0]):
            js = slice(j * LANE, (j + 1) * LANE)
            thr_ref[j, h] = thr[:, js]
            a_ref[j, h] = aw[:, js]
            s2_ref[j, h] = s2[:, js]
            e2_ref[j, h] = e2[:, js]


def _router(h2, lw, st):
    n, d = h2.shape
    tm = min(st["tm"], 256)
    nk = PEER_NKEYS
    half = PEER_QDIM // 2
    oshape = jax.ShapeDtypeStruct((n // LANE, PEER_HEADS, nk, LANE), F32)
    ospec = pl.BlockSpec((tm // LANE, PEER_HEADS, nk, LANE), lambda i: (i, 0, 0, 0))
    return pl.pallas_call(
        _router_kernel,
        grid=(n // tm,),
        in_specs=[
            pl.BlockSpec((tm, d), lambda i: (i, 0)),
            pl.BlockSpec((d, PEER_HEADS * PEER_QDIM), lambda i: (0, 0), pipeline_mode=pl.Buffered(1)),
            _const_spec((nk, half)),
            _const_spec((nk, half)),
        ],
        out_specs=[ospec, ospec, ospec, ospec],
        out_shape=[oshape, oshape, oshape, oshape],
        scratch_shapes=[pltpu.VMEM((-(-(PEER_TOPK + 1) // 8) * 8, tm), F32)],
        compiler_params=_cparams(("parallel",)),
        name="peer_router",
    )(h2, lw["peer_wq"], lw["k1"], lw["k2"])


def _peer_kernel(h2t_ref, u_ref, vt_ref, thr_ref, a_ref, s2_ref, e2_ref, x1_ref, gt2_ref, gf_ref,
                 o_ref, acc_scr, p_scr, *, te, final):
    s = pl.program_id(1)
    last = pl.num_programs(1) - 1
    e = jnp.minimum(s, last - 1)
    cur = s % 2
    ni = te // PEER_NKEYS
    nk = PEER_NKEYS
    d, tb = h2t_ref.shape
    dq = d // ni

    @pl.when(s == 0)
    def _():
        acc_scr[...] = jnp.zeros(acc_scr.shape, F32)
        p_scr[1] = jnp.zeros(p_scr.shape[1:], BF16)

    def body(ii, carry):
        r0 = pl.multiple_of(ii * nk, nk)
        d0 = pl.multiple_of(ii * dq, dq)
        pre = _dot(u_ref[pl.ds(r0, nk), :], h2t_ref[...])
        acc_scr[pl.ds(d0, dq), :] += _dot(vt_ref[pl.ds(d0, dq), :], p_scr[1 - cur])
        act = 0.5 * pre * (1.0 + lax.erf(pre * (2.0 ** -0.5)))
        for t in range(tb // LANE):
            ls = slice(t * LANE, (t + 1) * LANE)
            w = None
            for h in range(PEER_HEADS):
                thr = thr_ref[t, h, pl.ds(e * ni + ii, 1), :]
                arow = a_ref[t, h, pl.ds(e * ni + ii, 1), :]
                wh = jnp.where(s2_ref[t, h] >= thr, arow * e2_ref[t, h], 0.0)
                w = wh if w is None else w + wh
            p_scr[cur, pl.ds(r0, nk), ls] = (w * act[:, ls]).astype(BF16)
        return carry

    lax.fori_loop(0, ni, body, 0)

    @pl.when(s == last)
    def _():
        x2 = x1_ref[...] + gt2_ref[0] * acc_scr[...].T
        o_ref[...] = _rms(x2, gf_ref[...]) if final else x2


def _peer(h2t, rt, x1, mod3, lw, g_final, st, final):
    n, d = x1.shape
    tb, rpg, r = st["tb"], st["rpg"], st["r"]
    ne = lw["u"].shape[0]
    te = 512
    nk = PEER_NKEYS

    def gi(i):
        return (i * tb) // rpg

    sspec = pl.BlockSpec((tb // LANE, PEER_HEADS, nk, LANE), lambda i, e: (i, 0, 0, 0))
    kern = functools.partial(_peer_kernel, te=te, final=final)
    nt = ne // te
    return pl.pallas_call(
        kern,
        grid=(n // tb, nt + 1),
        in_specs=[
            pl.BlockSpec((d, tb), lambda i, e: (0, i)),
            pl.BlockSpec((te, d), lambda i, e: (jnp.minimum(e, nt - 1), 0)),
            pl.BlockSpec((d, te), lambda i, e: (0, jnp.maximum(e - 1, 0))),
            sspec, sspec, sspec, sspec,
            pl.BlockSpec((tb, d), lambda i, e: (i, 0)),
            pl.BlockSpec((1, r, d), lambda i, e: (gi(i), 0, 5)),
            pl.BlockSpec((1, d), lambda i, e: (0, 0)),
        ],
        out_specs=pl.BlockSpec((tb, d), lambda i, e: (i, 0)),
        out_shape=jax.ShapeDtypeStruct((n, d), F32),
        scratch_shapes=[pltpu.VMEM((d, tb), F32), pltpu.VMEM((2, te, tb), BF16)],
        compiler_params=_cparams(("parallel", "arbitrary")),
        name="peer_dense",
    )(h2t, lw["u"], lw["vT"], rt[0], rt[1], rt[2], rt[3], x1, mod3, g_final)


def _prep_layer(l, w_in, g_norm1, g_qnorm, w_uq, g_kvnorm, w_uk, w_uv, w_a, lbs, g_hg_onorm, w_b,
                w_out, g_norm2, peer_wq, peer_k1, peer_k2, peer_u, peer_v):
    d = w_in.shape[1]
    wi = w_in[l]
    o_kr = MLA_Q_RANK + MLA_KV_RANK
    o_hq = o_kr + MLA_ROPE
    hw = HG_HEADS * HG_K
    o_ga = o_hq + 4 * hw
    assert (o_hq, hw, o_ga + 2 * d, d) == (832, 1024, 9024, 2048)
    kr = wi[:, o_kr:o_hq]
    half = MLA_ROPE // 2
    kr_sw = jnp.concatenate([kr[:, half:], kr[:, :half]], axis=1)
    a_pad = jnp.zeros((d, COL_HQ - COL_A - o_hq - MLA_ROPE), F32)
    w_in_p = jnp.concatenate(
        [wi[:, o_ga:o_ga + d], wi[:, o_ga + d:o_ga + 2 * d], wi[:, :o_hq], kr_sw, a_pad, wi[:, o_hq:o_ga]],
        axis=1).astype(BF16)
    uq = w_uq[l].reshape(MLA_Q_RANK, MLA_HEADS, MLA_NOPE + MLA_ROPE)
    w_nope = uq[:, :, :MLA_NOPE].reshape(MLA_Q_RANK, MLA_HEADS * MLA_NOPE)
    rope = uq[:, :, MLA_NOPE:]
    rope_sw = jnp.concatenate([rope[..., half:], rope[..., :half]], axis=-1)
    lb = lbs[l].reshape(1, hw)
    lb3 = jnp.concatenate([jnp.log(lb), jnp.log1p(-lb), 1.0 - lb], axis=0)
    return {
        "w_in_p": w_in_p,
        "g1": g_norm1[l].reshape(1, d),
        "g_q": g_qnorm[l].reshape(1, MLA_Q_RANK),
        "g_kv": g_kvnorm[l].reshape(1, MLA_KV_RANK),
        "w_nope": w_nope.astype(BF16),
        "w_rope": rope.reshape(MLA_Q_RANK, MLA_HEADS * MLA_ROPE).astype(BF16),
        "w_rope_sw": rope_sw.reshape(MLA_Q_RANK, MLA_HEADS * MLA_ROPE).astype(BF16),
        "w_ukT": jnp.transpose(w_uk[l], (1, 2, 0)).astype(BF16),
        "w_uv": jnp.transpose(w_uv[l], (1, 0, 2)).astype(BF16),
        "w_a": w_a[l].astype(BF16),
        "w_b": w_b[l].astype(BF16),
        "w_out": w_out[l].astype(BF16),
        "lb3": lb3,
        "g_hg": g_hg_onorm[l].reshape(1, HG_V),
        "g2": g_norm2[l].reshape(1, d),
        "peer_wq": peer_wq[l].astype(BF16),
        "k1": peer_k1[l].astype(BF16),
        "k2": peer_k2[l].astype(BF16),
        "u": peer_u[l].astype(BF16),
        "vT": peer_v[l].T.astype(BF16),
    }


def _rope_tables(pos):
    half = MLA_ROPE // 2
    freq = ROPE_THETA ** (-jnp.arange(half, dtype=F32) / half)
    ang = pos.astype(F32)[:, None] * freq[None, :]
    cos = jnp.cos(ang)
    sin = jnp.sin(ang)
    cos_t = jnp.tile(jnp.concatenate([cos, cos], axis=1), (1, MLA_HEADS))
    sin_t = jnp.tile(jnp.concatenate([-sin, sin], axis=1), (1, MLA_HEADS))
    return cos_t, sin_t


def kernel(x_prompt, x_sample, cache_ckv, cache_krope, state_hgrn, page_table, c_prompt, c_sample, w_ada, b_ada, g_norm1, w_in, g_qnorm, w_uq, g_kvnorm, w_uk, w_uv, w_a, hg_lb_logits, g_hg_onorm, w_b, w_out, g_norm2, peer_wq, peer_k1, peer_k2, peer_u, peer_v, g_final):
    bsz, t, d = x_prompt.shape
    nsq, ts, _ = x_sample.shape
    assert ts == 1
    depth = w_in.shape[0]
    n_p = bsz * t
    n_s = nsq

    lbs = jnp.cumsum(jax.nn.softmax(hg_lb_logits.astype(F32), axis=0), axis=0)
    lbs = lbs - lbs[0]
    layers = [
        _prep_layer(l, w_in, g_norm1, g_qnorm, w_uq, g_kvnorm, w_uk, w_uv, w_a, lbs, g_hg_onorm, w_b,
                    w_out, g_norm2, peer_wq, peer_k1, peer_k2, peer_u, peer_v)
        for l in range(depth)
    ]
    gf = g_final.reshape(1, d)

    mod = _ada_mod(jnp.concatenate([c_prompt, c_sample], axis=0), w_ada, b_ada)
    cos_p, sin_p = _rope_tables(jnp.arange(t))
    past_len = page_table.shape[1] * cache_ckv.shape[2]
    cos_s, sin_s = _rope_tables(past_len + jnp.arange(1))
    cache_krope_t = jnp.swapaxes(cache_krope, 2, 3)

    st_p = {"tm": _tile(t, 512), "tb": _tile(t, 512), "rpg": t, "r": 1}
    st_s = {"tm": n_s, "tb": n_s, "rpg": n_s, "r": n_s}

    xp = x_prompt.reshape(n_p, d)
    xs = x_sample.reshape(n_s, d)
    ckv_p, kr_p, stt_p, ckv_s, kr_s, stt_s = [], [], [], [], [], []
    for l in range(depth):
        lw = layers[l]
        final = l == depth - 1
        mod_p = mod[l, :bsz].reshape(bsz, 1, 6 * d)
        mod_s = mod[l, bsz:].reshape(1, n_s, 6 * d)

        z = _in_proj(xp, mod_p, lw["g1"], lw["w_in_p"], st_p)
        q, ckv, kr, kvb = _mla_prep(z, lw, cos_p, sin_p, st_p)
        att = _attn_prompt(q, kvb, lw["w_uv"], bsz, t)
        o_r, s_end = _hgrn_prompt(z, lw["lb3"], bsz, t)
        x1, h2, h2t = _merge(att, o_r, z, xp, mod_p, lw, st_p)
        rt = _router(h2, lw, st_p)
        xp = _peer(h2t, rt, x1, mod_p, lw, gf, st_p, final)
        ckv_p.append(ckv.reshape(bsz, t, MLA_KV_RANK))
        kr_p.append(kr.reshape(bsz, t, MLA_ROPE))
        stt_p.append(s_end)

        z = _in_proj(xs, mod_s, lw["g1"], lw["w_in_p"], st_s)
        q, ckv, kr, kvb = _mla_prep(z, lw, cos_s, sin_s, st_s)
        lat = _attn_sample(jnp.transpose(q, (1, 0, 2)), kvb, cache_ckv, cache_krope_t, page_table, l)
        att = _uv_proj(jnp.transpose(lat, (1, 0, 2)), lw["w_uv"])
        o_r, s_new = _hgrn_sample(z, lw["lb3"], state_hgrn[l])
        x1, h2, h2t = _merge(att, o_r, z, xs, mod_s, lw, st_s)
        rt = _router(h2, lw, st_s)
        xs = _peer(h2t, rt, x1, mod_s, lw, gf, st_s, final)
        ckv_s.append(ckv.reshape(n_s, 1, MLA_KV_RANK))
        kr_s.append(kr.reshape(n_s, 1, MLA_ROPE))
        stt_s.append(s_new)

    return (xp.reshape(bsz, t, d), xs.reshape(n_s, 1, d),
            jnp.stack(ckv_p), jnp.stack(kr_p), jnp.stack(stt_p),
            jnp.stack(ckv_s), jnp.stack(kr_s), jnp.stack(stt_s))
```

```python
import functools

import jax
import jax.numpy as jnp
from jax import lax
from jax.experimental import pallas as pl
from jax.experimental.pallas import tpu as pltpu

F32 = jnp.float32
BF16 = jnp.bfloat16

MLA_HEADS = 8
MLA_Q_RANK = 512
MLA_KV_RANK = 256
MLA_NOPE = 128
MLA_ROPE = 64
MLA_V = 128
MLA_SCALE = (MLA_NOPE + MLA_ROPE) ** -0.5
ROPE_THETA = 10000.0
HG_HEADS = 8
HG_K = 128
HG_V = 128
PEER_HEADS = 8
PEER_NKEYS = 128
PEER_QDIM = 256
PEER_TOPK = 16
EPS = 1e-6

QK_PAD = 384
HG_CHUNK = 64
HG_SUB = 16
VMEM_LIMIT = 56 * 1024 * 1024
ATT_TQ, ATT_TK, ATT_GP = 256, 512, 32
LANE, SUBLANE = 128, 8
PEER_ROWS = 32
PEER_MM_LANES = 256

COL_GA, COL_GB, COL_A, COL_HQ, COL_HF, COL_HI, COL_HG, IN_PAD = 0, 2048, 4096, 5120, 6144, 7168, 8192, 9216


def _cparams(sem, flags=None):
    return pltpu.CompilerParams(dimension_semantics=sem, vmem_limit_bytes=VMEM_LIMIT, flags=flags)


def _tile(n, pref):
    t = min(n, pref)
    while n % t:
        t -= 8
    return t


def _dot(a, b):
    return jnp.dot(a, b, preferred_element_type=F32)


def _dot_nt(a, b):
    return lax.dot_general(a, b, (((1,), (1,)), ((), ())), preferred_element_type=F32)


def _dot_tn(a, b):
    return lax.dot_general(a, b, (((0,), (0,)), ((), ())), preferred_element_type=F32)


def _split3(x):
    hi = x.astype(BF16)
    r = x - hi.astype(F32)
    mid = r.astype(BF16)
    lo = (r - mid.astype(F32)).astype(BF16)
    return hi, mid, lo


def _rms(x, g):
    return x * lax.rsqrt(jnp.mean(x * x, axis=-1, keepdims=True) + EPS) * g


def _const_spec(shape):
    nd = len(shape)
    return pl.BlockSpec(shape, lambda *_: (0,) * nd)


def _ada_kernel(c_ref, w_ref, b_ref, o_ref):
    c = c_ref[...]
    cs = c * jax.nn.sigmoid(c)
    a_hi, a_mid, _ = _split3(cs)
    w_hi, w_mid, _ = _split3(w_ref[0])
    o_ref[0] = _dot(a_hi, w_hi) + _dot(a_mid, w_hi) + _dot(a_hi, w_mid) + b_ref[0]


def _ada_mod(c_all, w_ada, b_ada):
    depth, d, w6 = w_ada.shape
    r = c_all.shape[0]
    tn = 1024
    return pl.pallas_call(
        _ada_kernel,
        grid=(depth, w6 // tn),
        in_specs=[
            pl.BlockSpec((r, d), lambda l, j: (0, 0)),
            pl.BlockSpec((1, d, tn), lambda l, j: (l, 0, j)),
            pl.BlockSpec((1, 1, tn), lambda l, j: (l, 0, j)),
        ],
        out_specs=pl.BlockSpec((1, r, tn), lambda l, j: (l, 0, j)),
        out_shape=jax.ShapeDtypeStruct((depth, r, w6), F32),
        compiler_params=_cparams(("parallel", "parallel")),
        name="ada_mod",
    )(c_all, w_ada, b_ada.reshape(depth, 1, w6))


def _in_proj_kernel(x_ref, sc_ref, sh_ref, g_ref, w_ref, o_ref, h_scr):
    @pl.when(pl.program_id(1) == 0)
    def _():
        h = _rms(x_ref[...], g_ref[...]) * (1.0 + sc_ref[0]) + sh_ref[0]
        h_scr[...] = h.astype(BF16)

    o_ref[...] = _dot(h_scr[...], w_ref[...])


def _in_proj(x, mod3, g1, w_in_p, st):
    n, d = x.shape
    tm, rpg, r = st["tm_in"], st["rpg"], st["r"]
    tn = 1024
    width = w_in_p.shape[1]

    def gi(i):
        return (i * tm) // rpg

    return pl.pallas_call(
        _in_proj_kernel,
        grid=(n // tm, width // tn),
        in_specs=[
            pl.BlockSpec((tm, d), lambda i, j: (i, 0)),
            pl.BlockSpec((1, r, d), lambda i, j: (gi(i), 0, 1)),
            pl.BlockSpec((1, r, d), lambda i, j: (gi(i), 0, 0)),
            pl.BlockSpec((1, d), lambda i, j: (0, 0)),
            pl.BlockSpec((d, tn), lambda i, j: (0, j)),
        ],
        out_specs=pl.BlockSpec((tm, tn), lambda i, j: (i, j)),
        out_shape=jax.ShapeDtypeStruct((n, width), F32),
        scratch_shapes=[pltpu.VMEM((tm, d), BF16)],
        compiler_params=_cparams(("parallel", "arbitrary")),
        name="in_proj",
    )(x, mod3, mod3, g1, w_in_p)


def _mla_prep_kernel(z_ref, gq_ref, gkv_ref, wn_ref, wr_ref, wrs_ref, wuk_ref, cos_ref, sin_ref,
                     q_ref, ckv_ref, kr_ref, kvb_ref):
    z = z_ref[...]
    cq = z[:, :MLA_Q_RANK]
    ckv = z[:, MLA_Q_RANK:MLA_Q_RANK + MLA_KV_RANK]
    o = MLA_Q_RANK + MLA_KV_RANK
    kr = z[:, o:o + MLA_ROPE]
    krs = z[:, o + MLA_ROPE:o + 2 * MLA_ROPE]
    cos = cos_ref[...]
    sin = sin_ref[...]

    cqn = _rms(cq, gq_ref[...]).astype(BF16)
    qn = _dot(cqn, wn_ref[...])
    qrope = (_dot(cqn, wr_ref[...]) * cos + _dot(cqn, wrs_ref[...]) * sin) * MLA_SCALE
    tm = z.shape[0]
    zpad = jnp.zeros((tm, QK_PAD - MLA_KV_RANK - MLA_ROPE), BF16)
    for h in range(MLA_HEADS):
        ql = _dot(qn[:, h * MLA_NOPE:(h + 1) * MLA_NOPE].astype(BF16), wuk_ref[h]) * MLA_SCALE
        q_ref[h, :, 0:MLA_KV_RANK] = ql.astype(BF16)
        q_ref[h, :, MLA_KV_RANK:MLA_KV_RANK + MLA_ROPE] = qrope[:, h * MLA_ROPE:(h + 1) * MLA_ROPE].astype(BF16)
        q_ref[h, :, MLA_KV_RANK + MLA_ROPE:QK_PAD] = zpad

    ckvn = _rms(ckv, gkv_ref[...])
    kro = kr * cos[:, :MLA_ROPE] + krs * sin[:, :MLA_ROPE]
    ckv_ref[...] = ckvn
    kr_ref[...] = kro
    kvb_ref[:, 0:MLA_KV_RANK] = ckvn.astype(BF16)
    kvb_ref[:, MLA_KV_RANK:MLA_KV_RANK + MLA_ROPE] = kro.astype(BF16)
    kvb_ref[:, MLA_KV_RANK + MLA_ROPE:QK_PAD] = zpad


def _mla_prep(z, lw, cos_t, sin_t, st):
    n = z.shape[0]
    tm = st["tm"]
    rt = cos_t.shape[0]
    hr = MLA_HEADS * MLA_ROPE
    if rt == 1:
        tspec = pl.BlockSpec((1, hr), lambda i: (0, 0))
    else:
        nt = rt // tm
        tspec = pl.BlockSpec((tm, hr), lambda i: (i % nt, 0))
    return pl.pallas_call(
        _mla_prep_kernel,
        grid=(n // tm,),
        in_specs=[
            pl.BlockSpec((tm, 1024), lambda i: (i, COL_A // 1024)),
            _const_spec((1, MLA_Q_RANK)),
            _const_spec((1, MLA_KV_RANK)),
            _const_spec((MLA_Q_RANK, MLA_HEADS * MLA_NOPE)),
            _const_spec((MLA_Q_RANK, hr)),
            _const_spec((MLA_Q_RANK, hr)),
            _const_spec((MLA_HEADS, MLA_NOPE, MLA_KV_RANK)),
            tspec,
            tspec,
        ],
        out_specs=[
            pl.BlockSpec((MLA_HEADS, tm, QK_PAD), lambda i: (0, i, 0)),
            pl.BlockSpec((tm, MLA_KV_RANK), lambda i: (i, 0)),
            pl.BlockSpec((tm, MLA_ROPE), lambda i: (i, 0)),
            pl.BlockSpec((tm, QK_PAD), lambda i: (i, 0)),
        ],
        out_shape=[
            jax.ShapeDtypeStruct((MLA_HEADS, n, QK_PAD), BF16),
            jax.ShapeDtypeStruct((n, MLA_KV_RANK), F32),
            jax.ShapeDtypeStruct((n, MLA_ROPE), F32),
            jax.ShapeDtypeStruct((n, QK_PAD), BF16),
        ],
        compiler_params=_cparams(("parallel",)),
        name="mla_prep",
    )(z, lw["g_q"], lw["g_kv"], lw["w_nope"], lw["w_rope"], lw["w_rope_sw"], lw["w_ukT"], cos_t, sin_t)


def _attn_prompt_kernel(q_ref, k_ref, wuv_ref, o_ref, m_scr, l_scr, acc_scr, *, tq, tk):
    i = pl.program_id(1)
    j = pl.program_id(2)
    nh = MLA_HEADS

    @pl.when(j == 0)
    def _():
        m_scr[...] = jnp.full(m_scr.shape, -jnp.inf, F32)
        l_scr[...] = jnp.zeros(l_scr.shape, F32)
        acc_scr[...] = jnp.zeros(acc_scr.shape, F32)

    def block(masked):
        k = k_ref[...]
        v = k[:, :MLA_KV_RANK]
        if masked:
            qpos = i * tq + lax.broadcasted_iota(jnp.int32, (tq, tk), 0)
            kpos = j * tk + lax.broadcasted_iota(jnp.int32, (tq, tk), 1)
            keep = kpos <= qpos
        for h in range(nh):
            s = _dot_nt(q_ref[h], k)
            if masked:
                s = jnp.where(keep, s, -jnp.inf)
            m_prev = m_scr[h]
            m_new = jnp.maximum(m_prev, jnp.max(s, axis=-1, keepdims=True))
            corr = jnp.exp(m_prev - m_new)
            p = jnp.exp(s - m_new)
            l_scr[h] = l_scr[h] * corr + jnp.sum(p, axis=-1, keepdims=True)
            acc_scr[h] = acc_scr[h] * corr + _dot(p.astype(BF16), v)
            m_scr[h] = m_new

    @pl.when(j * tk + tk - 1 <= i * tq)
    def _():
        block(False)

    @pl.when(jnp.logical_and(j * tk + tk - 1 > i * tq, j * tk <= i * tq + tq - 1))
    def _():
        block(True)

    @pl.when(j == pl.num_programs(2) - 1)
    def _():
        for h in range(nh):
            lat = acc_scr[h] / l_scr[h]
            o_ref[:, h * MLA_V:(h + 1) * MLA_V] = _dot(lat.astype(BF16), wuv_ref[h]).astype(BF16)


def _attn_prompt(q, kvb, wuv, bsz, t):
    n = kvb.shape[0]
    tq = _tile(t, ATT_TQ)
    tk = _tile(t, ATT_TK)
    nq, nk = t // tq, t // tk
    kern = functools.partial(_attn_prompt_kernel, tq=tq, tk=tk)
    return pl.pallas_call(
        kern,
        grid=(bsz, nq, nk),
        in_specs=[
            pl.BlockSpec((MLA_HEADS, tq, QK_PAD), lambda b, i, j: (0, b * nq + i, 0)),
            pl.BlockSpec((tk, QK_PAD), lambda b, i, j: (b * nk + jnp.minimum(j, (i * tq + tq - 1) // tk), 0)),
            _const_spec((MLA_HEADS, MLA_KV_RANK, MLA_V)),
        ],
        out_specs=pl.BlockSpec((tq, MLA_HEADS * MLA_V), lambda b, i, j: (b * nq + i, 0)),
        out_shape=jax.ShapeDtypeStruct((n, MLA_HEADS * MLA_V), BF16),
        scratch_shapes=[
            pltpu.VMEM((MLA_HEADS, tq, 1), F32),
            pltpu.VMEM((MLA_HEADS, tq, 1), F32),
            pltpu.VMEM((MLA_HEADS, tq, MLA_KV_RANK), F32),
        ],
        compiler_params=_cparams(("parallel", "parallel", "arbitrary")),
        name="attn_prompt",
    )(q, kvb, wuv)


def _attn_sample_kernel(pt_ref, q_ref, knew_ref, ckv_hbm, kr_hbm, o_ref,
                        kbuf, rbuf, sem, m_scr, l_scr, acc_scr, *, layer, gp, ng, npg, nsteps):
    b = pl.program_id(0)
    g = pl.program_id(1)
    step = b * ng + g
    slot = step % 2

    def copies(st, sl):
        base = (st // ng) * npg + (st % ng) * gp
        cps = []
        for k in range(gp):
            page = pt_ref[base + k]
            cps.append(pltpu.make_async_copy(ckv_hbm.at[layer, page], kbuf.at[sl, k], sem.at[0, sl]))
            cps.append(pltpu.make_async_copy(kr_hbm.at[layer, page], rbuf.at[sl, k], sem.at[1, sl]))
        return cps

    @pl.when(step == 0)
    def _():
        for c in copies(step, slot):
            c.start()

    @pl.when(step + 1 < nsteps)
    def _():
        for c in copies(step + 1, 1 - slot):
            c.start()

    for c in copies(step, slot):
        c.wait()

    @pl.when(g == 0)
    def _():
        m_scr[...] = jnp.full(m_scr.shape, -jnp.inf, F32)
        l_scr[...] = jnp.zeros(l_scr.shape, F32)
        acc_scr[...] = jnp.zeros(acc_scr.shape, F32)

    q = q_ref[0]
    ps = kbuf.shape[2]
    kc = kbuf[slot].reshape(gp * ps, MLA_KV_RANK).astype(BF16)
    qr = q[:, MLA_KV_RANK:MLA_KV_RANK + MLA_ROPE]
    s_rope = jnp.concatenate([_dot(qr, rbuf[slot, k].astype(BF16)) for k in range(gp)], axis=1)
    s = _dot_nt(q[:, :MLA_KV_RANK], kc) + s_rope
    m_prev = m_scr[...]
    m_new = jnp.maximum(m_prev, jnp.max(s, axis=-1, keepdims=True))
    corr = jnp.exp(m_prev - m_new)
    p = jnp.exp(s - m_new)
    l_scr[...] = l_scr[...] * corr + jnp.sum(p, axis=-1, keepdims=True)
    acc_scr[...] = acc_scr[...] * corr + _dot(p.astype(BF16), kc)
    m_scr[...] = m_new

    @pl.when(g == ng - 1)
    def _():
        knew = knew_ref[0].astype(F32)
        s_new = jnp.sum(q.astype(F32) * knew, axis=-1, keepdims=True)
        m_old = m_scr[...]
        m_fin = jnp.maximum(m_old, s_new)
        c_old = jnp.exp(m_old - m_fin)
        p_new = jnp.exp(s_new - m_fin)
        l_fin = l_scr[...] * c_old + p_new
        acc = acc_scr[...] * c_old + p_new * knew[:, :MLA_KV_RANK]
        o_ref[0] = acc / l_fin


def _attn_sample(q_nh, kvb, cache_ckv, cache_krope_t, page_table, layer):
    n = q_nh.shape[0]
    npg = page_table.shape[1]
    ps = cache_ckv.shape[2]
    gp = min(npg, ATT_GP)
    while npg % gp:
        gp -= 1
    ng = npg // gp
    kern = functools.partial(_attn_sample_kernel, layer=layer, gp=gp, ng=ng, npg=npg, nsteps=n * ng)
    grid_spec = pltpu.PrefetchScalarGridSpec(
        num_scalar_prefetch=1,
        grid=(n, ng),
        in_specs=[
            pl.BlockSpec((1, MLA_HEADS, QK_PAD), lambda b, g, pt: (b, 0, 0)),
            pl.BlockSpec((1, 1, QK_PAD), lambda b, g, pt: (b, 0, 0)),
            pl.BlockSpec(memory_space=pl.ANY),
            pl.BlockSpec(memory_space=pl.ANY),
        ],
        out_specs=pl.BlockSpec((1, MLA_HEADS, MLA_KV_RANK), lambda b, g, pt: (b, 0, 0)),
        scratch_shapes=[
            pltpu.VMEM((2, gp, ps, MLA_KV_RANK), F32),
            pltpu.VMEM((2, gp, MLA_ROPE, ps), F32),
            pltpu.SemaphoreType.DMA((2, 2)),
            pltpu.VMEM((MLA_HEADS, 1), F32),
            pltpu.VMEM((MLA_HEADS, 1), F32),
            pltpu.VMEM((MLA_HEADS, MLA_KV_RANK), F32),
        ],
    )
    return pl.pallas_call(
        kern,
        grid_spec=grid_spec,
        out_shape=jax.ShapeDtypeStruct((n, MLA_HEADS, MLA_KV_RANK), F32),
        compiler_params=_cparams(("arbitrary", "arbitrary")),
        name="attn_sample",
    )(page_table.reshape(-1), q_nh, kvb.reshape(n, 1, QK_PAD), cache_ckv, cache_krope_t)


def _uv_proj_kernel(lat_ref, wuv_ref, o_ref):
    for h in range(MLA_HEADS):
        o_ref[:, h * MLA_V:(h + 1) * MLA_V] = _dot(lat_ref[h].astype(BF16), wuv_ref[h]).astype(BF16)


def _uv_proj(lat_hn, wuv):
    n = lat_hn.shape[1]
    return pl.pallas_call(
        _uv_proj_kernel,
        grid=(1,),
        in_specs=[_const_spec(lat_hn.shape), _const_spec(wuv.shape)],
        out_specs=_const_spec((n, MLA_HEADS * MLA_V)),
        out_shape=jax.ShapeDtypeStruct((n, MLA_HEADS * MLA_V), BF16),
        compiler_params=_cparams(("arbitrary",)),
        name="uv_proj",
    )(lat_hn, wuv)


def _hg_gates(fp, lb_ref):
    lsig = jnp.minimum(fp, 0.0) - jnp.log1p(jnp.exp(-jnp.abs(fp)))
    bterm = lb_ref[1:2, :] + lsig
    loglb = lb_ref[0:1, :]
    logf = jnp.maximum(loglb, bterm) + jnp.log1p(jnp.exp(-jnp.abs(loglb - bterm)))
    kk = lb_ref[2:3, :] * jax.nn.sigmoid(-fp)
    return logf, kk


def _hgrn_prompt_kernel(q_ref, f_ref, i_ref, lb_ref, o_ref, s_ref, st_scr, *, c, sb):
    ci = pl.program_id(1)
    nsb = c // sb
    kd = HG_K

    @pl.when(ci == 0)
    def _():
        st_scr[...] = jnp.zeros(st_scr.shape, F32)

    logf, kk_all = _hg_gates(f_ref[...], lb_ref)
    tril = (lax.broadcasted_iota(jnp.int32, (c, c), 0) >= lax.broadcasted_iota(jnp.int32, (c, c), 1)).astype(BF16)
    l_hi, l_mid, l_lo = _split3(logf)
    cum_all = _dot(tril, l_hi) + _dot(tril, l_mid) + _dot(tril, l_lo)
    q_all = q_ref[...]
    v_all = i_ref[...]
    ones = jnp.ones((kd, kd), BF16)
    tpos = lax.broadcasted_iota(jnp.int32, (nsb, sb, kd), 1)

    for h in range(HG_HEADS):
        hs = slice(h * kd, (h + 1) * kd)
        q = q_all[:, hs]
        k = kk_all[:, hs]
        v = v_all[:, hs]
        cu = cum_all[:, hs]
        s0 = st_scr[h]
        vb = v.astype(BF16)

        o_inter = _dot((q * jnp.exp(cu)).astype(BF16), s0.astype(BF16))
        blocks = [o_inter[i * sb:(i + 1) * sb] for i in range(nsb)]

        for j in range(nsb - 1):
            r0 = (j + 1) * sb
            bj = cu[r0 - 1:r0, :]
            ke = (k[j * sb:r0] * jnp.exp(bj - cu[j * sb:r0])).astype(BF16)
            qe = (q[r0:] * jnp.exp(cu[r0:] - bj)).astype(BF16)
            a = _dot_nt(qe, ke)
            lower = _dot(a.astype(BF16), vb[j * sb:r0])
            for i in range(j + 1, nsb):
                blocks[i] = blocks[i] + lower[(i - j - 1) * sb:(i - j) * sb]

        q4 = q.reshape(nsb, sb, kd)
        k4 = k.reshape(nsb, sb, kd)
        v4 = v.reshape(nsb, sb, kd)
        cu4 = cu.reshape(nsb, sb, kd)
        od = jnp.zeros((nsb, sb, kd), F32)
        for s in range(sb):
            d = cu4 - cu4[:, s:s + 1, :]
            w = jnp.where(tpos >= s, jnp.exp(d), 0.0)
            zz = (q4 * w * k4[:, s:s + 1, :]).reshape(c, kd).astype(BF16)
            r = _dot(zz, ones).reshape(nsb, sb, kd)
            od = od + r * v4[:, s:s + 1, :]

        o_ref[:, hs] = jnp.concatenate(blocks, axis=0) + od.reshape(c, kd)

        c_end = cu[c - 1:c, :]
        kdec = (k * jnp.exp(c_end - cu)).astype(BF16)
        upd = _dot_tn(kdec, vb)
        e_col = jnp.broadcast_to(jnp.exp(c_end), (kd, kd)).T
        st_scr[h] = e_col * s0 + upd

    @pl.when(ci == pl.num_programs(1) - 1)
    def _():
        s_ref[0] = st_scr[...]


def _hgrn_prompt(z, lb3, bsz, t):
    n = z.shape[0]
    c = _tile(t, HG_CHUNK)
    sb = min(HG_SUB, c)
    nc = t // c
    w = HG_HEADS * HG_K
    kern = functools.partial(_hgrn_prompt_kernel, c=c, sb=sb)
    return pl.pallas_call(
        kern,
        grid=(bsz, nc),
        in_specs=[
            pl.BlockSpec((c, w), lambda b, ci: (b * nc + ci, COL_HQ // w)),
            pl.BlockSpec((c, w), lambda b, ci: (b * nc + ci, COL_HF // w)),
            pl.BlockSpec((c, w), lambda b, ci: (b * nc + ci, COL_HI // w)),
            _const_spec((3, w)),
        ],
        out_specs=[
            pl.BlockSpec((c, w), lambda b, ci: (b * nc + ci, 0)),
            pl.BlockSpec((1, HG_HEADS, HG_K, HG_V), lambda b, ci: (b, 0, 0, 0)),
        ],
        out_shape=[
            jax.ShapeDtypeStruct((n, w), F32),
            jax.ShapeDtypeStruct((bsz, HG_HEADS, HG_K, HG_V), F32),
        ],
        scratch_shapes=[pltpu.VMEM((HG_HEADS, HG_K, HG_V), F32)],
        compiler_params=_cparams(("parallel", "arbitrary")),
        name="hgrn_prompt",
    )(z, z, z, lb3)


def _hgrn_sample_kernel(q_ref, f_ref, i_ref, lb_ref, s0_ref, o_ref, s_ref, *, tb):
    kd = HG_K
    logf, kk_all = _hg_gates(f_ref[...], lb_ref)
    f_all = jnp.exp(logf)
    q_all = q_ref[...]
    v_all = i_ref[...]
    for b in range(tb):
        for h in range(HG_HEADS):
            hs = slice(h * kd, (h + 1) * kd)
            e_f = jnp.broadcast_to(f_all[b:b + 1, hs], (kd, kd)).T
            e_k = jnp.broadcast_to(kk_all[b:b + 1, hs], (kd, kd)).T
            s_new = e_f * s0_ref[b, h] + e_k * v_all[b:b + 1, hs]
            s_ref[b, h] = s_new
            qrow = jnp.broadcast_to(q_all[b:b + 1, hs], (8, kd)).astype(BF16)
            o_ref[b:b + 1, hs] = _dot(qrow, s_new.astype(BF16))[0:1]


def _hgrn_sample(z, lb3, state, layer):
    n = z.shape[0]
    tb = 8
    w = HG_HEADS * HG_K
    kern = functools.partial(_hgrn_sample_kernel, tb=tb)
    return pl.pallas_call(
        kern,
        grid=(n // tb,),
        in_specs=[
            pl.BlockSpec((tb, w), lambda i: (i, COL_HQ // w)),
            pl.BlockSpec((tb, w), lambda i: (i, COL_HF // w)),
            pl.BlockSpec((tb, w), lambda i: (i, COL_HI // w)),
            _const_spec((3, w)),
            pl.BlockSpec((None, tb, HG_HEADS, HG_K, HG_V), lambda i: (layer, i, 0, 0, 0)),
        ],
        out_specs=[
            pl.BlockSpec((tb, w), lambda i: (i, 0)),
            pl.BlockSpec((tb, HG_HEADS, HG_K, HG_V), lambda i: (i, 0, 0, 0)),
        ],
        out_shape=[
            jax.ShapeDtypeStruct((n, w), F32),
            jax.ShapeDtypeStruct(state.shape[1:], F32),
        ],
        compiler_params=_cparams(("parallel",)),
        name="hgrn_sample",
    )(z, z, z, lb3, state)


def _merge_kernel(att_ref, or_ref, hg_ref, ga_ref, gb_ref, x_ref, gt1_ref, sc2_ref, sh2_ref,
                  ghg_ref, g2_ref, wa_ref, wb_ref, wo_ref, x1_ref, h2_ref, h2t_ref):
    o_r = or_ref[...]
    hg = hg_ref[...]
    ghg = ghg_ref[...]
    parts = []
    for h in range(HG_HEADS):
        hs = slice(h * HG_V, (h + 1) * HG_V)
        g = hg[:, hs]
        parts.append((_rms(o_r[:, hs], ghg) * (g * jax.nn.sigmoid(g))).astype(BF16))
    orn = jnp.concatenate(parts, axis=-1)
    oa = _dot(att_ref[...], wa_ref[...])
    ob = _dot(orn, wb_ref[...])
    m = jax.nn.sigmoid(ga_ref[...]) * oa + jax.nn.sigmoid(gb_ref[...]) * ob
    y = _dot(m.astype(BF16), wo_ref[...])
    x1 = x_ref[...] + gt1_ref[0] * y
    x1_ref[...] = x1
    h2 = _rms(x1, g2_ref[...]) * (1.0 + sc2_ref[0]) + sh2_ref[0]
    h2_ref[...] = h2.astype(BF16)
    h2t_ref[...] = h2.T.astype(BF16)


def _merge(att, o_r, z, x, mod3, lw, st):
    n, d = x.shape
    tm, rpg, r = min(st["tm"], 256), st["rpg"], st["r"]
    w = HG_HEADS * HG_V

    def gi(i):
        return (i * tm) // rpg

    def mspec(k):
        return pl.BlockSpec((1, r, d), lambda i: (gi(i), 0, k))

    one = pl.Buffered(1)
    return pl.pallas_call(
        _merge_kernel,
        grid=(n // tm,),
        in_specs=[
            pl.BlockSpec((tm, w), lambda i: (i, 0)),
            pl.BlockSpec((tm, w), lambda i: (i, 0)),
            pl.BlockSpec((tm, w), lambda i: (i, COL_HG // w)),
            pl.BlockSpec((tm, d), lambda i: (i, COL_GA // d)),
            pl.BlockSpec((tm, d), lambda i: (i, COL_GB // d)),
            pl.BlockSpec((tm, d), lambda i: (i, 0)),
            mspec(2), mspec(4), mspec(3),
            _const_spec((1, HG_V)),
            _const_spec((1, d)),
            pl.BlockSpec((w, d), lambda i: (0, 0), pipeline_mode=one),
            pl.BlockSpec((w, d), lambda i: (0, 0), pipeline_mode=one),
            pl.BlockSpec((d, d), lambda i: (0, 0), pipeline_mode=one),
        ],
        out_specs=[
            pl.BlockSpec((tm, d), lambda i: (i, 0)),
            pl.BlockSpec((tm, d), lambda i: (i, 0)),
            pl.BlockSpec((d, tm), lambda i: (0, i)),
        ],
        out_shape=[
            jax.ShapeDtypeStruct((n, d), F32),
            jax.ShapeDtypeStruct((n, d), BF16),
            jax.ShapeDtypeStruct((d, n), BF16),
        ],
        compiler_params=_cparams(("parallel",)),
        name="merge",
    )(att, o_r, z, z, z, x, mod3, mod3, mod3, lw["g_hg"], lw["g2"], lw["w_a"], lw["w_b"], lw["w_out"])


def _top_sorted(s, k):
    vals = []
    for _ in range(k):
        m = jnp.max(s, axis=0, keepdims=True)
        vals.append(m)
        s = jnp.where(s == m, -jnp.inf, s)
    return vals


def _router_kernel(h2_ref, wq_ref, k1_ref, k2_ref, thr_ref, a_ref, s2_ref, e2_ref, v2_scr):
    qp = _dot(h2_ref[...], wq_ref[...])
    half = PEER_QDIM // 2
    k1 = k1_ref[...]
    k2 = k2_ref[...]
    topk = PEER_TOPK
    nx = topk + 1
    for h in range(PEER_HEADS):
        qa = qp[:, h * PEER_QDIM:h * PEER_QDIM + half].astype(BF16)
        qb = qp[:, h * PEER_QDIM + half:(h + 1) * PEER_QDIM].astype(BF16)
        s1 = _dot_nt(k1, qa)
        s2 = _dot_nt(k2, qb)
        v1 = _top_sorted(s1, nx)
        v2 = _top_sorted(s2, nx)
        v2_scr[...] = jnp.full(v2_scr.shape, -jnp.inf, F32)
        for a in range(nx):
            v2_scr[a:a + 1, :] = v2[a]
        pieces = []
        for a in range(nx):
            nb = -(-(nx // (a + 1)) // 8) * 8
            pieces.append(v1[a] + v2_scr[0:nb, :])
        cand = jnp.concatenate(pieces, axis=0)
        sc = _top_sorted(cand, nx)
        mid = 0.5 * (sc[topk - 1] + sc[topk])
        zsum = jnp.zeros_like(mid)
        for a in range(topk):
            zsum = zsum + jnp.exp(sc[a] - sc[0])
        thr = mid - s1
        aw = jnp.exp(s1 - v1[0]) / zsum
        e2 = jnp.exp(s2 - v2[0])
        for j in range(thr_ref.shape[0]):
            js = slice(j * LANE, (j + 1) * LANE)
            thr_ref[j, h] = thr[:, js]
            a_ref[j, h] = aw[:, js]
            s2_ref[j, h] = s2[:, js]
            e2_ref[j, h] = e2[:, js]


def _router(h2, lw, st):
    n, d = h2.shape
    tm = min(st["tm"], 256)
    nk = PEER_NKEYS
    half = PEER_QDIM // 2
    oshape = (n // LANE, PEER_HEADS, nk, LANE)
    ospec = pl.BlockSpec((tm // LANE, PEER_HEADS, nk, LANE), lambda i: (i, 0, 0, 0))
    return pl.pallas_call(
        _router_kernel,
        grid=(n // tm,),
        in_specs=[
            pl.BlockSpec((tm, d), lambda i: (i, 0)),
            pl.BlockSpec((d, PEER_HEADS * PEER_QDIM), lambda i: (0, 0), pipeline_mode=pl.Buffered(1)),
            _const_spec((nk, half)),
            _const_spec((nk, half)),
        ],
        out_specs=[ospec, ospec, ospec, ospec],
        out_shape=[jax.ShapeDtypeStruct(oshape, F32)] * 4,
        scratch_shapes=[pltpu.VMEM((-(-(PEER_TOPK + 1) // 8) * 8, tm), F32)],
        compiler_params=_cparams(("parallel",)),
        name="peer_router",
    )(h2, lw["peer_wq"], lw["k1"], lw["k2"])


def _peer_kernel(h2t_ref, u_ref, vt_ref, thr_ref, a_ref, s2_ref, e2_ref, x1_ref, gt2_ref, gf_ref,
                 o_ref, acc_scr, pre_scr, p_scr, *, te, final):
    e = pl.program_id(1)
    ni = te // PEER_NKEYS
    nk = PEER_NKEYS
    tb = h2t_ref.shape[1]

    @pl.when(e == 0)
    def _():
        acc_scr[...] = jnp.zeros(acc_scr.shape, F32)

    pre_scr[...] = _dot(u_ref[...], h2t_ref[...])
    for t in range(tb // LANE):
        ls = slice(t * LANE, (t + 1) * LANE)
        for ii in range(ni):
            w = None
            for h in range(PEER_HEADS):
                thr = thr_ref[t, h, pl.ds(e * ni + ii, 1), :]
                arow = a_ref[t, h, pl.ds(e * ni + ii, 1), :]
                wh = jnp.where(s2_ref[t, h] >= thr, arow * e2_ref[t, h], 0.0)
                w = wh if w is None else w + wh
            pre = pre_scr[ii * nk:(ii + 1) * nk, ls]
            act = 0.5 * pre * (1.0 + lax.erf(pre * (2.0 ** -0.5)))
            p_scr[ii * nk:(ii + 1) * nk, ls] = (w * act).astype(BF16)
    acc_scr[...] += _dot(vt_ref[...], p_scr[...])

    @pl.when(e == pl.num_programs(1) - 1)
    def _():
        x2 = x1_ref[...] + gt2_ref[0] * acc_scr[...].T
        o_ref[...] = _rms(x2, gf_ref[...]) if final else x2


def _peer(h2t, rt, x1, mod3, u_all, vt_all, layer, g_final, st, final):
    n, d = x1.shape
    tb, rpg, r = st["tb"], st["rpg"], st["r"]
    ne = u_all.shape[1]
    te = 512
    nk = PEER_NKEYS

    def gi(i):
        return (i * tb) // rpg

    sspec = pl.BlockSpec((tb // LANE, PEER_HEADS, nk, LANE), lambda i, e: (i, 0, 0, 0))
    kern = functools.partial(_peer_kernel, te=te, final=final)
    return pl.pallas_call(
        kern,
        grid=(n // tb, ne // te),
        in_specs=[
            pl.BlockSpec((d, tb), lambda i, e: (0, i)),
            pl.BlockSpec((None, te, d), lambda i, e: (layer, e, 0)),
            pl.BlockSpec((None, d, te), lambda i, e: (layer, 0, e)),
            sspec, sspec, sspec, sspec,
            pl.BlockSpec((tb, d), lambda i, e: (i, 0)),
            pl.BlockSpec((1, r, d), lambda i, e: (gi(i), 0, 5)),
            pl.BlockSpec((1, d), lambda i, e: (0, 0)),
        ],
        out_specs=pl.BlockSpec((tb, d), lambda i, e: (i, 0)),
        out_shape=jax.ShapeDtypeStruct((n, d), F32),
        scratch_shapes=[pltpu.VMEM((d, tb), F32), pltpu.VMEM((te, tb), F32), pltpu.VMEM((te, tb), BF16)],
        compiler_params=_cparams(("parallel", "arbitrary")),
        name="peer_dense",
    )(h2t, u_all, vt_all, rt[0], rt[1], rt[2], rt[3], x1, mod3, g_final)


def _prep_layer(l, w_in, g_norm1, g_qnorm, w_uq, g_kvnorm, w_uk, w_uv, w_a, lbs, g_hg_onorm, w_b,
                w_out, g_norm2, peer_wq, peer_k1, peer_k2, peer_u, peer_v):
    d = w_in.shape[1]
    wi = w_in[l]
    o_kr = MLA_Q_RANK + MLA_KV_RANK
    o_hq = o_kr + MLA_ROPE
    hw = HG_HEADS * HG_K
    o_ga = o_hq + 4 * hw
    assert (o_hq, hw, o_ga + 2 * d, d) == (832, 1024, 9024, 2048)
    kr = wi[:, o_kr:o_hq]
    half = MLA_ROPE // 2
    kr_sw = jnp.concatenate([kr[:, half:], kr[:, :half]], axis=1)
    a_pad = jnp.zeros((d, COL_HQ - COL_A - o_hq - MLA_ROPE), F32)
    w_in_p = jnp.concatenate(
        [wi[:, o_ga:o_ga + d], wi[:, o_ga + d:o_ga + 2 * d], wi[:, :o_hq], kr_sw, a_pad, wi[:, o_hq:o_ga]],
        axis=1).astype(BF16)
    uq = w_uq[l].reshape(MLA_Q_RANK, MLA_HEADS, MLA_NOPE + MLA_ROPE)
    w_nope = uq[:, :, :MLA_NOPE].reshape(MLA_Q_RANK, MLA_HEADS * MLA_NOPE)
    rope = uq[:, :, MLA_NOPE:]
    rope_sw = jnp.concatenate([rope[..., half:], rope[..., :half]], axis=-1)
    lb = lbs[l].reshape(1, hw)
    lb3 = jnp.concatenate([jnp.log(lb), jnp.log1p(-lb), 1.0 - lb], axis=0)
    return {
        "w_in_p": w_in_p,
        "g1": g_norm1[l].reshape(1, d),
        "g_q": g_qnorm[l].reshape(1, MLA_Q_RANK),
        "g_kv": g_kvnorm[l].reshape(1, MLA_KV_RANK),
        "w_nope": w_nope.astype(BF16),
        "w_rope": rope.reshape(MLA_Q_RANK, MLA_HEADS * MLA_ROPE).astype(BF16),
        "w_rope_sw": rope_sw.reshape(MLA_Q_RANK, MLA_HEADS * MLA_ROPE).astype(BF16),
        "w_ukT": jnp.transpose(w_uk[l], (1, 2, 0)).astype(BF16),
        "w_uv": jnp.transpose(w_uv[l], (1, 0, 2)).astype(BF16),
        "w_a": w_a[l].astype(BF16),
        "w_b": w_b[l].astype(BF16),
        "w_out": w_out[l].astype(BF16),
        "lb3": lb3,
        "g_hg": g_hg_onorm[l].reshape(1, HG_V),
        "g2": g_norm2[l].reshape(1, d),
        "peer_wq": peer_wq[l].astype(BF16),
        "k1": peer_k1[l].astype(BF16),
        "k2": peer_k2[l].astype(BF16),
    }


def _rope_tables(pos):
    half = MLA_ROPE // 2
    freq = ROPE_THETA ** (-jnp.arange(half, dtype=F32) / half)
    ang = pos.astype(F32)[:, None] * freq[None, :]
    cos = jnp.cos(ang)
    sin = jnp.sin(ang)
    cos_t = jnp.tile(jnp.concatenate([cos, cos], axis=1), (1, MLA_HEADS))
    sin_t = jnp.tile(jnp.concatenate([-sin, sin], axis=1), (1, MLA_HEADS))
    return cos_t, sin_t


def kernel(x_prompt, x_sample, cache_ckv, cache_krope, state_hgrn, page_table, c_prompt, c_sample, w_ada, b_ada, g_norm1, w_in, g_qnorm, w_uq, g_kvnorm, w_uk, w_uv, w_a, hg_lb_logits, g_hg_onorm, w_b, w_out, g_norm2, peer_wq, peer_k1, peer_k2, peer_u, peer_v, g_final):
    bsz, t, d = x_prompt.shape
    nsq, ts, _ = x_sample.shape
    assert ts == 1
    depth = w_in.shape[0]
    n_p = bsz * t
    n_s = nsq

    lbs = jnp.cumsum(jax.nn.softmax(hg_lb_logits.astype(F32), axis=0), axis=0)
    lbs = lbs - lbs[0]
    layers = [
        _prep_layer(l, w_in, g_norm1, g_qnorm, w_uq, g_kvnorm, w_uk, w_uv, w_a, lbs, g_hg_onorm, w_b,
                    w_out, g_norm2, peer_wq, peer_k1, peer_k2, peer_u, peer_v)
        for l in range(depth)
    ]
    gf = g_final.reshape(1, d)

    mod = _ada_mod(jnp.concatenate([c_prompt, c_sample], axis=0), w_ada, b_ada)
    cos_p, sin_p = _rope_tables(jnp.arange(t))
    past_len = page_table.shape[1] * cache_ckv.shape[2]
    cos_s, sin_s = _rope_tables(past_len + jnp.arange(1))
    cache_krope_t = jnp.swapaxes(cache_krope, 2, 3)
    u_all = peer_u.astype(BF16)
    vt_all = jnp.swapaxes(peer_v, 1, 2).astype(BF16)

    st_p = {"tm": _tile(t, 512), "tm_in": _tile(t, 1024), "tb": _tile(t, 512), "rpg": t, "r": 1}
    st_s = {"tm": n_s, "tm_in": n_s, "tb": n_s, "rpg": n_s, "r": n_s}

    xp = x_prompt.reshape(n_p, d)
    xs = x_sample.reshape(n_s, d)
    ckv_p, kr_p, stt_p, ckv_s, kr_s, stt_s = [], [], [], [], [], []
    for l in range(depth):
        lw = layers[l]
        final = l == depth - 1
        mod_p = mod[l, :bsz].reshape(bsz, 1, 6 * d)
        mod_s = mod[l, bsz:].reshape(1, n_s, 6 * d)

        z = _in_proj(xp, mod_p, lw["g1"], lw["w_in_p"], st_p)
        q, ckv, kr, kvb = _mla_prep(z, lw, cos_p, sin_p, st_p)
        att = _attn_prompt(q, kvb, lw["w_uv"], bsz, t)
        o_r, s_end = _hgrn_prompt(z, lw["lb3"], bsz, t)
        x1, h2, h2t = _merge(att, o_r, z, xp, mod_p, lw, st_p)
        rt = _router(h2, lw, st_p)
        xp = _peer(h2t, rt, x1, mod_p, u_all, vt_all, l, gf, st_p, final)
        ckv_p.append(ckv.reshape(bsz, t, MLA_KV_RANK))
        kr_p.append(kr.reshape(bsz, t, MLA_ROPE))
        stt_p.append(s_end)

        z = _in_proj(xs, mod_s, lw["g1"], lw["w_in_p"], st_s)
        q, ckv, kr, kvb = _mla_prep(z, lw, cos_s, sin_s, st_s)
        lat = _attn_sample(jnp.transpose(q, (1, 0, 2)), kvb, cache_ckv, cache_krope_t, page_table, l)
        att = _uv_proj(jnp.transpose(lat, (1, 0, 2)), lw["w_uv"])
        o_r, s_new = _hgrn_sample(z, lw["lb3"], state_hgrn, l)
        x1, h2, h2t = _merge(att, o_r, z, xs, mod_s, lw, st_s)
        rt = _router(h2, lw, st_s)
        xs = _peer(h2t, rt, x1, mod_s, u_all, vt_all, l, gf, st_s, final)
        ckv_s.append(ckv.reshape(n_s, 1, MLA_KV_RANK))
        kr_s.append(kr.reshape(n_s, 1, MLA_ROPE))
        stt_s.append(s_new)

    return (xp.reshape(bsz, t, d), xs.reshape(n_s, 1, d),
            jnp.stack(ckv_p), jnp.stack(kr_p), jnp.stack(stt_p),
            jnp.stack(ckv_s), jnp.stack(kr_s), jnp.stack(stt_s))
```

```python
import functools

import jax
import jax.numpy as jnp
from jax import lax
from jax.experimental import pallas as pl
from jax.experimental.pallas import tpu as pltpu

F32 = jnp.float32
BF16 = jnp.bfloat16

MLA_HEADS = 8
MLA_Q_RANK = 512
MLA_KV_RANK = 256
MLA_NOPE = 128
MLA_ROPE = 64
MLA_V = 128
MLA_SCALE = (MLA_NOPE + MLA_ROPE) ** -0.5
ROPE_THETA = 10000.0
HG_HEADS = 8
HG_K = 128
HG_V = 128
PEER_HEADS = 8
PEER_NKEYS = 128
PEER_QDIM = 256
PEER_TOPK = 16
EPS = 1e-6

QK_PAD = 384
HG_CHUNK = 64
HG_SUB = 16
VMEM_LIMIT = 56 * 1024 * 1024
ATT_TQ, ATT_TK, ATT_GP = 256, 512, 32
LANE, SUBLANE = 128, 8
PEER_TE = 512
PEER_MM_LANES = 256

COL_GA, COL_GB, COL_A, COL_HQ, COL_HF, COL_HI, COL_HG, IN_PAD = 0, 2048, 4096, 5120, 6144, 7168, 8192, 9216


def _cparams(sem, flags=None):
    return pltpu.CompilerParams(dimension_semantics=sem, vmem_limit_bytes=VMEM_LIMIT, flags=flags)


def _tile(n, pref):
    t = min(n, pref)
    while n % t:
        t -= 8
    return t


def _dot(a, b):
    return jnp.dot(a, b, preferred_element_type=F32)


def _dot_nt(a, b):
    return lax.dot_general(a, b, (((1,), (1,)), ((), ())), preferred_element_type=F32)


def _dot_tn(a, b):
    return lax.dot_general(a, b, (((0,), (0,)), ((), ())), preferred_element_type=F32)


def _split3(x):
    hi = x.astype(BF16)
    r = x - hi.astype(F32)
    mid = r.astype(BF16)
    lo = (r - mid.astype(F32)).astype(BF16)
    return hi, mid, lo


def _rms(x, g):
    return x * lax.rsqrt(jnp.mean(x * x, axis=-1, keepdims=True) + EPS) * g


def _const_spec(shape):
    nd = len(shape)
    return pl.BlockSpec(shape, lambda *_: (0,) * nd)


def _ada_kernel(c_ref, w_ref, b_ref, o_ref):
    c = c_ref[...]
    cs = c * jax.nn.sigmoid(c)
    a_hi, a_mid, _ = _split3(cs)
    w_hi, w_mid, _ = _split3(w_ref[0])
    o_ref[0] = _dot(a_hi, w_hi) + _dot(a_mid, w_hi) + _dot(a_hi, w_mid) + b_ref[0]


def _ada_mod(c_all, w_ada, b_ada):
    depth, d, w6 = w_ada.shape
    r = c_all.shape[0]
    tn = 1024
    return pl.pallas_call(
        _ada_kernel,
        grid=(depth, w6 // tn),
        in_specs=[
            pl.BlockSpec((r, d), lambda l, j: (0, 0)),
            pl.BlockSpec((1, d, tn), lambda l, j: (l, 0, j)),
            pl.BlockSpec((1, 1, tn), lambda l, j: (l, 0, j)),
        ],
        out_specs=pl.BlockSpec((1, r, tn), lambda l, j: (l, 0, j)),
        out_shape=jax.ShapeDtypeStruct((depth, r, w6), F32),
        compiler_params=_cparams(("parallel", "parallel")),
        name="ada_mod",
    )(c_all, w_ada, b_ada.reshape(depth, 1, w6))


def _in_proj_kernel(x_ref, sc_ref, sh_ref, g_ref, w_ref, o_ref, h_scr):
    @pl.when(pl.program_id(1) == 0)
    def _():
        h = _rms(x_ref[...], g_ref[...]) * (1.0 + sc_ref[0]) + sh_ref[0]
        h_scr[...] = h.astype(BF16)

    o_ref[...] = _dot(h_scr[...], w_ref[...])


def _in_proj(x, mod3, g1, w_in_p, st):
    n, d = x.shape
    tm, rpg, r = st["tm_in"], st["rpg"], st["r"]
    tn = 1024
    width = w_in_p.shape[1]

    def gi(i):
        return (i * tm) // rpg

    return pl.pallas_call(
        _in_proj_kernel,
        grid=(n // tm, width // tn),
        in_specs=[
            pl.BlockSpec((tm, d), lambda i, j: (i, 0)),
            pl.BlockSpec((1, r, d), lambda i, j: (gi(i), 0, 1)),
            pl.BlockSpec((1, r, d), lambda i, j: (gi(i), 0, 0)),
            pl.BlockSpec((1, d), lambda i, j: (0, 0)),
            pl.BlockSpec((d, tn), lambda i, j: (0, j)),
        ],
        out_specs=pl.BlockSpec((tm, tn), lambda i, j: (i, j)),
        out_shape=jax.ShapeDtypeStruct((n, width), F32),
        scratch_shapes=[pltpu.VMEM((tm, d), BF16)],
        compiler_params=_cparams(("parallel", "arbitrary")),
        name="in_proj",
    )(x, mod3, mod3, g1, w_in_p)


def _mla_prep_kernel(z_ref, gq_ref, gkv_ref, wn_ref, wr_ref, wrs_ref, wuk_ref, cos_ref, sin_ref,
                     q_ref, ckv_ref, kr_ref, kvb_ref):
    z = z_ref[...]
    cq = z[:, :MLA_Q_RANK]
    ckv = z[:, MLA_Q_RANK:MLA_Q_RANK + MLA_KV_RANK]
    o = MLA_Q_RANK + MLA_KV_RANK
    kr = z[:, o:o + MLA_ROPE]
    krs = z[:, o + MLA_ROPE:o + 2 * MLA_ROPE]
    cos = cos_ref[...]
    sin = sin_ref[...]

    cqn = _rms(cq, gq_ref[...]).astype(BF16)
    qn = _dot(cqn, wn_ref[...])
    qrope = (_dot(cqn, wr_ref[...]) * cos + _dot(cqn, wrs_ref[...]) * sin) * MLA_SCALE
    tm = z.shape[0]
    zpad = jnp.zeros((tm, QK_PAD - MLA_KV_RANK - MLA_ROPE), BF16)
    for h in range(MLA_HEADS):
        ql = _dot(qn[:, h * MLA_NOPE:(h + 1) * MLA_NOPE].astype(BF16), wuk_ref[h]) * MLA_SCALE
        q_ref[h, :, 0:MLA_KV_RANK] = ql.astype(BF16)
        q_ref[h, :, MLA_KV_RANK:MLA_KV_RANK + MLA_ROPE] = qrope[:, h * MLA_ROPE:(h + 1) * MLA_ROPE].astype(BF16)
        q_ref[h, :, MLA_KV_RANK + MLA_ROPE:QK_PAD] = zpad

    ckvn = _rms(ckv, gkv_ref[...])
    kro = kr * cos[:, :MLA_ROPE] + krs * sin[:, :MLA_ROPE]
    ckv_ref[...] = ckvn
    kr_ref[...] = kro
    kvb_ref[:, 0:MLA_KV_RANK] = ckvn.astype(BF16)
    kvb_ref[:, MLA_KV_RANK:MLA_KV_RANK + MLA_ROPE] = kro.astype(BF16)
    kvb_ref[:, MLA_KV_RANK + MLA_ROPE:QK_PAD] = zpad


def _mla_prep(z, lw, cos_t, sin_t, st):
    n = z.shape[0]
    tm = st["tm"]
    rt = cos_t.shape[0]
    hr = MLA_HEADS * MLA_ROPE
    if rt == 1:
        tspec = pl.BlockSpec((1, hr), lambda i: (0, 0))
    else:
        nt = rt // tm
        tspec = pl.BlockSpec((tm, hr), lambda i: (i % nt, 0))
    return pl.pallas_call(
        _mla_prep_kernel,
        grid=(n // tm,),
        in_specs=[
            pl.BlockSpec((tm, 1024), lambda i: (i, COL_A // 1024)),
            _const_spec((1, MLA_Q_RANK)),
            _const_spec((1, MLA_KV_RANK)),
            _const_spec((MLA_Q_RANK, MLA_HEADS * MLA_NOPE)),
            _const_spec((MLA_Q_RANK, hr)),
            _const_spec((MLA_Q_RANK, hr)),
            _const_spec((MLA_HEADS, MLA_NOPE, MLA_KV_RANK)),
            tspec,
            tspec,
        ],
        out_specs=[
            pl.BlockSpec((MLA_HEADS, tm, QK_PAD), lambda i: (0, i, 0)),
            pl.BlockSpec((tm, MLA_KV_RANK), lambda i: (i, 0)),
            pl.BlockSpec((tm, MLA_ROPE), lambda i: (i, 0)),
            pl.BlockSpec((tm, QK_PAD), lambda i: (i, 0)),
        ],
        out_shape=[
            jax.ShapeDtypeStruct((MLA_HEADS, n, QK_PAD), BF16),
            jax.ShapeDtypeStruct((n, MLA_KV_RANK), F32),
            jax.ShapeDtypeStruct((n, MLA_ROPE), F32),
            jax.ShapeDtypeStruct((n, QK_PAD), BF16),
        ],
        compiler_params=_cparams(("parallel",)),
        name="mla_prep",
    )(z, lw["g_q"], lw["g_kv"], lw["w_nope"], lw["w_rope"], lw["w_rope_sw"], lw["w_ukT"], cos_t, sin_t)


def _attn_prompt_kernel(q_ref, k_ref, wuv_ref, o_ref, m_scr, l_scr, acc_scr, *, tq, tk):
    i = pl.program_id(1)
    j = pl.program_id(2)
    nh = MLA_HEADS

    @pl.when(j == 0)
    def _():
        m_scr[...] = jnp.full(m_scr.shape, -jnp.inf, F32)
        l_scr[...] = jnp.zeros(l_scr.shape, F32)
        acc_scr[...] = jnp.zeros(acc_scr.shape, F32)

    def block(masked):
        k = k_ref[...]
        v = k[:, :MLA_KV_RANK]
        if masked:
            qpos = i * tq + lax.broadcasted_iota(jnp.int32, (tq, tk), 0)
            kpos = j * tk + lax.broadcasted_iota(jnp.int32, (tq, tk), 1)
            keep = kpos <= qpos
        for h in range(nh):
            s = _dot_nt(q_ref[h], k)
            if masked:
                s = jnp.where(keep, s, -jnp.inf)
            m_prev = m_scr[h]
            m_new = jnp.maximum(m_prev, jnp.max(s, axis=-1, keepdims=True))
            corr = jnp.exp(m_prev - m_new)
            p = jnp.exp(s - m_new)
            l_scr[h] = l_scr[h] * corr + jnp.sum(p, axis=-1, keepdims=True)
            acc_scr[h] = acc_scr[h] * corr + _dot(p.astype(BF16), v)
            m_scr[h] = m_new

    @pl.when(j * tk + tk - 1 <= i * tq)
    def _():
        block(False)

    @pl.when(jnp.logical_and(j * tk + tk - 1 > i * tq, j * tk <= i * tq + tq - 1))
    def _():
        block(True)

    @pl.when(j == pl.num_programs(2) - 1)
    def _():
        for h in range(nh):
            lat = acc_scr[h] / l_scr[h]
            o_ref[:, h * MLA_V:(h + 1) * MLA_V] = _dot(lat.astype(BF16), wuv_ref[h]).astype(BF16)


def _attn_prompt(q, kvb, wuv, bsz, t):
    n = kvb.shape[0]
    tq = _tile(t, ATT_TQ)
    tk = _tile(t, ATT_TK)
    nq, nk = t // tq, t // tk
    kern = functools.partial(_attn_prompt_kernel, tq=tq, tk=tk)
    return pl.pallas_call(
        kern,
        grid=(bsz, nq, nk),
        in_specs=[
            pl.BlockSpec((MLA_HEADS, tq, QK_PAD), lambda b, i, j: (0, b * nq + i, 0)),
            pl.BlockSpec((tk, QK_PAD), lambda b, i, j: (b * nk + jnp.minimum(j, (i * tq + tq - 1) // tk), 0)),
            _const_spec((MLA_HEADS, MLA_KV_RANK, MLA_V)),
        ],
        out_specs=pl.BlockSpec((tq, MLA_HEADS * MLA_V), lambda b, i, j: (b * nq + i, 0)),
        out_shape=jax.ShapeDtypeStruct((n, MLA_HEADS * MLA_V), BF16),
        scratch_shapes=[
            pltpu.VMEM((MLA_HEADS, tq, 1), F32),
            pltpu.VMEM((MLA_HEADS, tq, 1), F32),
            pltpu.VMEM((MLA_HEADS, tq, MLA_KV_RANK), F32),
        ],
        compiler_params=_cparams(("parallel", "parallel", "arbitrary")),
        name="attn_prompt",
    )(q, kvb, wuv)


def _attn_sample_kernel(pt_ref, q_ref, knew_ref, ckv_hbm, kr_hbm, o_ref,
                        kbuf, rbuf, sem, m_scr, l_scr, acc_scr, *, layer, gp, ng, npg, nsteps):
    b = pl.program_id(0)
    g = pl.program_id(1)
    step = b * ng + g
    slot = step % 2

    def copies(st, sl):
        base = (st // ng) * npg + (st % ng) * gp
        cps = []
        for k in range(gp):
            page = pt_ref[base + k]
            cps.append(pltpu.make_async_copy(ckv_hbm.at[layer, page], kbuf.at[sl, k], sem.at[0, sl]))
            cps.append(pltpu.make_async_copy(kr_hbm.at[layer, page], rbuf.at[sl, k], sem.at[1, sl]))
        return cps

    @pl.when(step == 0)
    def _():
        for c in copies(step, slot):
            c.start()

    @pl.when(step + 1 < nsteps)
    def _():
        for c in copies(step + 1, 1 - slot):
            c.start()

    for c in copies(step, slot):
        c.wait()

    @pl.when(g == 0)
    def _():
        m_scr[...] = jnp.full(m_scr.shape, -jnp.inf, F32)
        l_scr[...] = jnp.zeros(l_scr.shape, F32)
        acc_scr[...] = jnp.zeros(acc_scr.shape, F32)

    q = q_ref[0]
    ps = kbuf.shape[2]
    kc = kbuf[slot].reshape(gp * ps, MLA_KV_RANK).astype(BF16)
    qr = q[:, MLA_KV_RANK:MLA_KV_RANK + MLA_ROPE]
    s_rope = jnp.concatenate([_dot(qr, rbuf[slot, k].astype(BF16)) for k in range(gp)], axis=1)
    s = _dot_nt(q[:, :MLA_KV_RANK], kc) + s_rope
    m_prev = m_scr[...]
    m_new = jnp.maximum(m_prev, jnp.max(s, axis=-1, keepdims=True))
    corr = jnp.exp(m_prev - m_new)
    p = jnp.exp(s - m_new)
    l_scr[...] = l_scr[...] * corr + jnp.sum(p, axis=-1, keepdims=True)
    acc_scr[...] = acc_scr[...] * corr + _dot(p.astype(BF16), kc)
    m_scr[...] = m_new

    @pl.when(g == ng - 1)
    def _():
        knew = knew_ref[0].astype(F32)
        s_new = jnp.sum(q.astype(F32) * knew, axis=-1, keepdims=True)
        m_old = m_scr[...]
        m_fin = jnp.maximum(m_old, s_new)
        c_old = jnp.exp(m_old - m_fin)
        p_new = jnp.exp(s_new - m_fin)
        l_fin = l_scr[...] * c_old + p_new
        acc = acc_scr[...] * c_old + p_new * knew[:, :MLA_KV_RANK]
        o_ref[0] = acc / l_fin


def _attn_sample(q_nh, kvb, cache_ckv, cache_krope_t, page_table, layer):
    n = q_nh.shape[0]
    npg = page_table.shape[1]
    ps = cache_ckv.shape[2]
    gp = min(npg, ATT_GP)
    while npg % gp:
        gp -= 1
    ng = npg // gp
    kern = functools.partial(_attn_sample_kernel, layer=layer, gp=gp, ng=ng, npg=npg, nsteps=n * ng)
    grid_spec = pltpu.PrefetchScalarGridSpec(
        num_scalar_prefetch=1,
        grid=(n, ng),
        in_specs=[
            pl.BlockSpec((1, MLA_HEADS, QK_PAD), lambda b, g, pt: (b, 0, 0)),
            pl.BlockSpec((1, 1, QK_PAD), lambda b, g, pt: (b, 0, 0)),
            pl.BlockSpec(memory_space=pl.ANY),
            pl.BlockSpec(memory_space=pl.ANY),
        ],
        out_specs=pl.BlockSpec((1, MLA_HEADS, MLA_KV_RANK), lambda b, g, pt: (b, 0, 0)),
        scratch_shapes=[
            pltpu.VMEM((2, gp, ps, MLA_KV_RANK), F32),
            pltpu.VMEM((2, gp, MLA_ROPE, ps), F32),
            pltpu.SemaphoreType.DMA((2, 2)),
            pltpu.VMEM((MLA_HEADS, 1), F32),
            pltpu.VMEM((MLA_HEADS, 1), F32),
            pltpu.VMEM((MLA_HEADS, MLA_KV_RANK), F32),
        ],
    )
    return pl.pallas_call(
        kern,
        grid_spec=grid_spec,
        out_shape=jax.ShapeDtypeStruct((n, MLA_HEADS, MLA_KV_RANK), F32),
        compiler_params=_cparams(("arbitrary", "arbitrary")),
        name="attn_sample",
    )(page_table.reshape(-1), q_nh, kvb.reshape(n, 1, QK_PAD), cache_ckv, cache_krope_t)


def _uv_proj_kernel(lat_ref, wuv_ref, o_ref):
    for h in range(MLA_HEADS):
        o_ref[:, h * MLA_V:(h + 1) * MLA_V] = _dot(lat_ref[h].astype(BF16), wuv_ref[h]).astype(BF16)


def _uv_proj(lat_hn, wuv):
    n = lat_hn.shape[1]
    return pl.pallas_call(
        _uv_proj_kernel,
        grid=(1,),
        in_specs=[_const_spec(lat_hn.shape), _const_spec(wuv.shape)],
        out_specs=_const_spec((n, MLA_HEADS * MLA_V)),
        out_shape=jax.ShapeDtypeStruct((n, MLA_HEADS * MLA_V), BF16),
        compiler_params=_cparams(("arbitrary",)),
        name="uv_proj",
    )(lat_hn, wuv)


def _hg_gates(fp, lb_ref):
    lsig = jnp.minimum(fp, 0.0) - jnp.log1p(jnp.exp(-jnp.abs(fp)))
    bterm = lb_ref[1:2, :] + lsig
    loglb = lb_ref[0:1, :]
    logf = jnp.maximum(loglb, bterm) + jnp.log1p(jnp.exp(-jnp.abs(loglb - bterm)))
    kk = lb_ref[2:3, :] * jax.nn.sigmoid(-fp)
    return logf, kk


def _hgrn_prompt_kernel(q_ref, f_ref, i_ref, lb_ref, o_ref, s_ref, st_scr, *, c, sb):
    ci = pl.program_id(1)
    nsb = c // sb
    kd = HG_K

    @pl.when(ci == 0)
    def _():
        st_scr[...] = jnp.zeros(st_scr.shape, F32)

    logf, kk_all = _hg_gates(f_ref[...], lb_ref)
    tril = (lax.broadcasted_iota(jnp.int32, (c, c), 0) >= lax.broadcasted_iota(jnp.int32, (c, c), 1)).astype(BF16)
    l_hi, l_mid, l_lo = _split3(logf)
    cum_all = _dot(tril, l_hi) + _dot(tril, l_mid) + _dot(tril, l_lo)
    q_all = q_ref[...]
    v_all = i_ref[...]
    ones = jnp.ones((kd, kd), BF16)
    tpos = lax.broadcasted_iota(jnp.int32, (nsb, sb, kd), 1)

    for h in range(HG_HEADS):
        hs = slice(h * kd, (h + 1) * kd)
        q = q_all[:, hs]
        k = kk_all[:, hs]
        v = v_all[:, hs]
        cu = cum_all[:, hs]
        s0 = st_scr[h]
        vb = v.astype(BF16)

        o_inter = _dot((q * jnp.exp(cu)).astype(BF16), s0.astype(BF16))
        blocks = [o_inter[i * sb:(i + 1) * sb] for i in range(nsb)]

        for j in range(nsb - 1):
            r0 = (j + 1) * sb
            bj = cu[r0 - 1:r0, :]
            ke = (k[j * sb:r0] * jnp.exp(bj - cu[j * sb:r0])).astype(BF16)
            qe = (q[r0:] * jnp.exp(cu[r0:] - bj)).astype(BF16)
            a = _dot_nt(qe, ke)
            lower = _dot(a.astype(BF16), vb[j * sb:r0])
            for i in range(j + 1, nsb):
                blocks[i] = blocks[i] + lower[(i - j - 1) * sb:(i - j) * sb]

        q4 = q.reshape(nsb, sb, kd)
        k4 = k.reshape(nsb, sb, kd)
        v4 = v.reshape(nsb, sb, kd)
        cu4 = cu.reshape(nsb, sb, kd)
        od = jnp.zeros((nsb, sb, kd), F32)
        for s in range(sb):
            d = cu4 - cu4[:, s:s + 1, :]
            w = jnp.where(tpos >= s, jnp.exp(d), 0.0)
            zz = (q4 * w * k4[:, s:s + 1, :]).reshape(c, kd).astype(BF16)
            r = _dot(zz, ones).reshape(nsb, sb, kd)
            od = od + r * v4[:, s:s + 1, :]

        o_ref[:, hs] = jnp.concatenate(blocks, axis=0) + od.reshape(c, kd)

        c_end = cu[c - 1:c, :]
        kdec = (k * jnp.exp(c_end - cu)).astype(BF16)
        upd = _dot_tn(kdec, vb)
        e_col = jnp.broadcast_to(jnp.exp(c_end), (kd, kd)).T
        st_scr[h] = e_col * s0 + upd

    @pl.when(ci == pl.num_programs(1) - 1)
    def _():
        s_ref[0] = st_scr[...]


def _hgrn_prompt(z, lb3, bsz, t):
    n = z.shape[0]
    c = _tile(t, HG_CHUNK)
    sb = min(HG_SUB, c)
    nc = t // c
    w = HG_HEADS * HG_K
    kern = functools.partial(_hgrn_prompt_kernel, c=c, sb=sb)
    return pl.pallas_call(
        kern,
        grid=(bsz, nc),
        in_specs=[
            pl.BlockSpec((c, w), lambda b, ci: (b * nc + ci, COL_HQ // w)),
            pl.BlockSpec((c, w), lambda b, ci: (b * nc + ci, COL_HF // w)),
            pl.BlockSpec((c, w), lambda b, ci: (b * nc + ci, COL_HI // w)),
            _const_spec((3, w)),
        ],
        out_specs=[
            pl.BlockSpec((c, w), lambda b, ci: (b * nc + ci, 0)),
            pl.BlockSpec((1, HG_HEADS, HG_K, HG_V), lambda b, ci: (b, 0, 0, 0)),
        ],
        out_shape=[
            jax.ShapeDtypeStruct((n, w), F32),
            jax.ShapeDtypeStruct((bsz, HG_HEADS, HG_K, HG_V), F32),
        ],
        scratch_shapes=[pltpu.VMEM((HG_HEADS, HG_K, HG_V), F32)],
        compiler_params=_cparams(("parallel", "arbitrary")),
        name="hgrn_prompt",
    )(z, z, z, lb3)


def _hgrn_sample_kernel(q_ref, f_ref, i_ref, lb_ref, s0_ref, o_ref, s_ref, *, tb):
    kd = HG_K
    logf, kk_all = _hg_gates(f_ref[...], lb_ref)
    f_all = jnp.exp(logf)
    q_all = q_ref[...]
    v_all = i_ref[...]
    for b in range(tb):
        for h in range(HG_HEADS):
            hs = slice(h * kd, (h + 1) * kd)
            e_f = jnp.broadcast_to(f_all[b:b + 1, hs], (kd, kd)).T
            e_k = jnp.broadcast_to(kk_all[b:b + 1, hs], (kd, kd)).T
            s_new = e_f * s0_ref[b, h] + e_k * v_all[b:b + 1, hs]
            s_ref[b, h] = s_new
            qrow = jnp.broadcast_to(q_all[b:b + 1, hs], (8, kd)).astype(BF16)
            o_ref[b:b + 1, hs] = _dot(qrow, s_new.astype(BF16))[0:1]


def _hgrn_sample(z, lb3, state, layer):
    n = z.shape[0]
    tb = 8
    w = HG_HEADS * HG_K
    kern = functools.partial(_hgrn_sample_kernel, tb=tb)
    return pl.pallas_call(
        kern,
        grid=(n // tb,),
        in_specs=[
            pl.BlockSpec((tb, w), lambda i: (i, COL_HQ // w)),
            pl.BlockSpec((tb, w), lambda i: (i, COL_HF // w)),
            pl.BlockSpec((tb, w), lambda i: (i, COL_HI // w)),
            _const_spec((3, w)),
            pl.BlockSpec((None, tb, HG_HEADS, HG_K, HG_V), lambda i: (layer, i, 0, 0, 0)),
        ],
        out_specs=[
            pl.BlockSpec((tb, w), lambda i: (i, 0)),
            pl.BlockSpec((tb, HG_HEADS, HG_K, HG_V), lambda i: (i, 0, 0, 0)),
        ],
        out_shape=[
            jax.ShapeDtypeStruct((n, w), F32),
            jax.ShapeDtypeStruct(state.shape[1:], F32),
        ],
        compiler_params=_cparams(("parallel",)),
        name="hgrn_sample",
    )(z, z, z, lb3, state)


def _merge_kernel(att_ref, or_ref, hg_ref, ga_ref, gb_ref, x_ref, gt1_ref, sc2_ref, sh2_ref,
                  ghg_ref, g2_ref, wa_ref, wb_ref, wo_ref, x1_ref, h2_ref, h2t_ref):
    o_r = or_ref[...]
    hg = hg_ref[...]
    ghg = ghg_ref[...]
    parts = []
    for h in range(HG_HEADS):
        hs = slice(h * HG_V, (h + 1) * HG_V)
        g = hg[:, hs]
        parts.append((_rms(o_r[:, hs], ghg) * (g * jax.nn.sigmoid(g))).astype(BF16))
    orn = jnp.concatenate(parts, axis=-1)
    oa = _dot(att_ref[...], wa_ref[...])
    ob = _dot(orn, wb_ref[...])
    m = jax.nn.sigmoid(ga_ref[...]) * oa + jax.nn.sigmoid(gb_ref[...]) * ob
    y = _dot(m.astype(BF16), wo_ref[...])
    x1 = x_ref[...] + gt1_ref[0] * y
    x1_ref[...] = x1
    h2 = _rms(x1, g2_ref[...]) * (1.0 + sc2_ref[0]) + sh2_ref[0]
    h2_ref[...] = h2.astype(BF16)
    h2t_ref[...] = h2.T.astype(BF16)


def _merge(att, o_r, z, x, mod3, lw, st):
    n, d = x.shape
    tm, rpg, r = min(st["tm"], 256), st["rpg"], st["r"]
    w = HG_HEADS * HG_V

    def gi(i):
        return (i * tm) // rpg

    def mspec(k):
        return pl.BlockSpec((1, r, d), lambda i: (gi(i), 0, k))

    one = pl.Buffered(1)
    return pl.pallas_call(
        _merge_kernel,
        grid=(n // tm,),
        in_specs=[
            pl.BlockSpec((tm, w), lambda i: (i, 0)),
            pl.BlockSpec((tm, w), lambda i: (i, 0)),
            pl.BlockSpec((tm, w), lambda i: (i, COL_HG // w)),
            pl.BlockSpec((tm, d), lambda i: (i, COL_GA // d)),
            pl.BlockSpec((tm, d), lambda i: (i, COL_GB // d)),
            pl.BlockSpec((tm, d), lambda i: (i, 0)),
            mspec(2), mspec(4), mspec(3),
            _const_spec((1, HG_V)),
            _const_spec((1, d)),
            pl.BlockSpec((w, d), lambda i: (0, 0), pipeline_mode=one),
            pl.BlockSpec((w, d), lambda i: (0, 0), pipeline_mode=one),
            pl.BlockSpec((d, d), lambda i: (0, 0), pipeline_mode=one),
        ],
        out_specs=[
            pl.BlockSpec((tm, d), lambda i: (i, 0)),
            pl.BlockSpec((tm, d), lambda i: (i, 0)),
            pl.BlockSpec((d, tm), lambda i: (0, i)),
        ],
        out_shape=[
            jax.ShapeDtypeStruct((n, d), F32),
            jax.ShapeDtypeStruct((n, d), BF16),
            jax.ShapeDtypeStruct((d, n), BF16),
        ],
        compiler_params=_cparams(("parallel",)),
        name="merge",
    )(att, o_r, z, z, z, x, mod3, mod3, mod3, lw["g_hg"], lw["g2"], lw["w_a"], lw["w_b"], lw["w_out"])


def _top_sorted(s, k):
    vals = []
    for _ in range(k):
        m = jnp.max(s, axis=0, keepdims=True)
        vals.append(m)
        s = jnp.where(s == m, -jnp.inf, s)
    return vals


def _router_kernel(h2_ref, wq_ref, k1_ref, k2_ref, thr_ref, a_ref, s2_ref, e2_ref, v2_scr):
    qp = _dot(h2_ref[...], wq_ref[...])
    half = PEER_QDIM // 2
    k1 = k1_ref[...]
    k2 = k2_ref[...]
    topk = PEER_TOPK
    nx = topk + 1
    for h in range(PEER_HEADS):
        qa = qp[:, h * PEER_QDIM:h * PEER_QDIM + half].astype(BF16)
        qb = qp[:, h * PEER_QDIM + half:(h + 1) * PEER_QDIM].astype(BF16)
        s1 = _dot_nt(k1, qa)
        s2 = _dot_nt(k2, qb)
        v1 = _top_sorted(s1, nx)
        v2 = _top_sorted(s2, nx)
        v2_scr[...] = jnp.full(v2_scr.shape, -jnp.inf, F32)
        for a in range(nx):
            v2_scr[a:a + 1, :] = v2[a]
        pieces = []
        for a in range(nx):
            nb = -(-(nx // (a + 1)) // 8) * 8
            pieces.append(v1[a] + v2_scr[0:nb, :])
        cand = jnp.concatenate(pieces, axis=0)
        sc = _top_sorted(cand, nx)
        mid = 0.5 * (sc[topk - 1] + sc[topk])
        zsum = jnp.zeros_like(mid)
        for a in range(topk):
            zsum = zsum + jnp.exp(sc[a] - sc[0])
        thr = mid - s1
        aw = jnp.exp(s1 - v1[0]) / zsum
        e2 = jnp.exp(s2 - v2[0])
        for j in range(thr_ref.shape[0]):
            js = slice(j * LANE, (j + 1) * LANE)
            thr_ref[j, h] = thr[:, js]
            a_ref[j, h] = aw[:, js]
            s2_ref[j, h] = s2[:, js]
            e2_ref[j, h] = e2[:, js]


def _router(h2, lw, st):
    n, d = h2.shape
    tm = min(st["tm"], 256)
    nk = PEER_NKEYS
    half = PEER_QDIM // 2
    oshape = (n // LANE, PEER_HEADS, nk, LANE)
    ospec = pl.BlockSpec((tm // LANE, PEER_HEADS, nk, LANE), lambda i: (i, 0, 0, 0))
    return pl.pallas_call(
        _router_kernel,
        grid=(n // tm,),
        in_specs=[
            pl.BlockSpec((tm, d), lambda i: (i, 0)),
            pl.BlockSpec((d, PEER_HEADS * PEER_QDIM), lambda i: (0, 0), pipeline_mode=pl.Buffered(1)),
            _const_spec((nk, half)),
            _const_spec((nk, half)),
        ],
        out_specs=[ospec, ospec, ospec, ospec],
        out_shape=[jax.ShapeDtypeStruct(oshape, F32)] * 4,
        scratch_shapes=[pltpu.VMEM((-(-(PEER_TOPK + 1) // 8) * 8, tm), F32)],
        compiler_params=_cparams(("parallel",)),
        name="peer_router",
    )(h2, lw["peer_wq"], lw["k1"], lw["k2"])


def _peer_kernel(h2t_ref, u_ref, vt_ref, thr_ref, a_ref, s2_ref, e2_ref, x1_ref, gt2_ref, gf_ref,
                 o_ref, acc_scr, pre_scr, p_scr, *, te, final):
    e = pl.program_id(1)
    ni = te // PEER_NKEYS
    nk = PEER_NKEYS
    tb = h2t_ref.shape[1]

    @pl.when(e == 0)
    def _():
        acc_scr[...] = jnp.zeros(acc_scr.shape, F32)

    pre_scr[...] = _dot(u_ref[...], h2t_ref[...])
    for t in range(tb // LANE):
        ls = slice(t * LANE, (t + 1) * LANE)
        for ii in range(ni):
            w = None
            for h in range(PEER_HEADS):
                thr = thr_ref[t, h, pl.ds(e * ni + ii, 1), :]
                arow = a_ref[t, h, pl.ds(e * ni + ii, 1), :]
                wh = jnp.where(s2_ref[t, h] >= thr, arow * e2_ref[t, h], 0.0)
                w = wh if w is None else w + wh
            pre = pre_scr[ii * nk:(ii + 1) * nk, ls]
            act = 0.5 * pre * (1.0 + lax.erf(pre * (2.0 ** -0.5)))
            p_scr[ii * nk:(ii + 1) * nk, ls] = (w * act).astype(BF16)
    acc_scr[...] += _dot(vt_ref[...], p_scr[...])

    @pl.when(e == pl.num_programs(1) - 1)
    def _():
        x2 = x1_ref[...] + gt2_ref[0] * acc_scr[...].T
        o_ref[...] = _rms(x2, gf_ref[...]) if final else x2


def _peer(h2t, rt, x1, mod3, u_all, vt_all, layer, g_final, st, final):
    n, d = x1.shape
    tb, rpg, r = st["tb"], st["rpg"], st["r"]
    ne = u_all.shape[1]
    te = PEER_TE
    nk = PEER_NKEYS

    def gi(i):
        return (i * tb) // rpg

    sspec = pl.BlockSpec((tb // LANE, PEER_HEADS, nk, LANE), lambda i, e: (i, 0, 0, 0))
    kern = functools.partial(_peer_kernel, te=te, final=final)
    return pl.pallas_call(
        kern,
        grid=(n // tb, ne // te),
        in_specs=[
            pl.BlockSpec((d, tb), lambda i, e: (0, i)),
            pl.BlockSpec((None, te, d), lambda i, e: (layer, e, 0)),
            pl.BlockSpec((None, None, d, te), lambda i, e: (layer, e, 0, 0)),
            sspec, sspec, sspec, sspec,
            pl.BlockSpec((tb, d), lambda i, e: (i, 0)),
            pl.BlockSpec((1, r, d), lambda i, e: (gi(i), 0, 5)),
            pl.BlockSpec((1, d), lambda i, e: (0, 0)),
        ],
        out_specs=pl.BlockSpec((tb, d), lambda i, e: (i, 0)),
        out_shape=jax.ShapeDtypeStruct((n, d), F32),
        scratch_shapes=[pltpu.VMEM((d, tb), F32), pltpu.VMEM((te, tb), F32), pltpu.VMEM((te, tb), BF16)],
        compiler_params=_cparams(("parallel", "arbitrary")),
        name="peer_dense",
    )(h2t, u_all, vt_all, rt[0], rt[1], rt[2], rt[3], x1, mod3, g_final)


def _prep_layer(l, w_in, g_norm1, g_qnorm, w_uq, g_kvnorm, w_uk, w_uv, w_a, lbs, g_hg_onorm, w_b,
                w_out, g_norm2, peer_wq, peer_k1, peer_k2, peer_u, peer_v):
    d = w_in.shape[1]
    wi = w_in[l]
    o_kr = MLA_Q_RANK + MLA_KV_RANK
    o_hq = o_kr + MLA_ROPE
    hw = HG_HEADS * HG_K
    o_ga = o_hq + 4 * hw
    assert (o_hq, hw, o_ga + 2 * d, d) == (832, 1024, 9024, 2048)
    kr = wi[:, o_kr:o_hq]
    half = MLA_ROPE // 2
    kr_sw = jnp.concatenate([kr[:, half:], kr[:, :half]], axis=1)
    a_pad = jnp.zeros((d, COL_HQ - COL_A - o_hq - MLA_ROPE), F32)
    w_in_p = jnp.concatenate(
        [wi[:, o_ga:o_ga + d], wi[:, o_ga + d:o_ga + 2 * d], wi[:, :o_hq], kr_sw, a_pad, wi[:, o_hq:o_ga]],
        axis=1).astype(BF16)
    uq = w_uq[l].reshape(MLA_Q_RANK, MLA_HEADS, MLA_NOPE + MLA_ROPE)
    w_nope = uq[:, :, :MLA_NOPE].reshape(MLA_Q_RANK, MLA_HEADS * MLA_NOPE)
    rope = uq[:, :, MLA_NOPE:]
    rope_sw = jnp.concatenate([rope[..., half:], rope[..., :half]], axis=-1)
    lb = lbs[l].reshape(1, hw)
    lb3 = jnp.concatenate([jnp.log(lb), jnp.log1p(-lb), 1.0 - lb], axis=0)
    return {
        "w_in_p": w_in_p,
        "g1": g_norm1[l].reshape(1, d),
        "g_q": g_qnorm[l].reshape(1, MLA_Q_RANK),
        "g_kv": g_kvnorm[l].reshape(1, MLA_KV_RANK),
        "w_nope": w_nope.astype(BF16),
        "w_rope": rope.reshape(MLA_Q_RANK, MLA_HEADS * MLA_ROPE).astype(BF16),
        "w_rope_sw": rope_sw.reshape(MLA_Q_RANK, MLA_HEADS * MLA_ROPE).astype(BF16),
        "w_ukT": jnp.transpose(w_uk[l], (1, 2, 0)).astype(BF16),
        "w_uv": jnp.transpose(w_uv[l], (1, 0, 2)).astype(BF16),
        "w_a": w_a[l].astype(BF16),
        "w_b": w_b[l].astype(BF16),
        "w_out": w_out[l].astype(BF16),
        "lb3": lb3,
        "g_hg": g_hg_onorm[l].reshape(1, HG_V),
        "g2": g_norm2[l].reshape(1, d),
        "peer_wq": peer_wq[l].astype(BF16),
        "k1": peer_k1[l].astype(BF16),
        "k2": peer_k2[l].astype(BF16),
    }


def _rope_tables(pos):
    half = MLA_ROPE // 2
    freq = ROPE_THETA ** (-jnp.arange(half, dtype=F32) / half)
    ang = pos.astype(F32)[:, None] * freq[None, :]
    cos = jnp.cos(ang)
    sin = jnp.sin(ang)
    cos_t = jnp.tile(jnp.concatenate([cos, cos], axis=1), (1, MLA_HEADS))
    sin_t = jnp.tile(jnp.concatenate([-sin, sin], axis=1), (1, MLA_HEADS))
    return cos_t, sin_t


def kernel(x_prompt, x_sample, cache_ckv, cache_krope, state_hgrn, page_table, c_prompt, c_sample, w_ada, b_ada, g_norm1, w_in, g_qnorm, w_uq, g_kvnorm, w_uk, w_uv, w_a, hg_lb_logits, g_hg_onorm, w_b, w_out, g_norm2, peer_wq, peer_k1, peer_k2, peer_u, peer_v, g_final):
    bsz, t, d = x_prompt.shape
    nsq, ts, _ = x_sample.shape
    assert ts == 1
    depth = w_in.shape[0]
    n_p = bsz * t
    n_s = nsq

    lbs = jnp.cumsum(jax.nn.softmax(hg_lb_logits.astype(F32), axis=0), axis=0)
    lbs = lbs - lbs[0]
    layers = [
        _prep_layer(l, w_in, g_norm1, g_qnorm, w_uq, g_kvnorm, w_uk, w_uv, w_a, lbs, g_hg_onorm, w_b,
                    w_out, g_norm2, peer_wq, peer_k1, peer_k2, peer_u, peer_v)
        for l in range(depth)
    ]
    gf = g_final.reshape(1, d)

    mod = _ada_mod(jnp.concatenate([c_prompt, c_sample], axis=0), w_ada, b_ada)
    cos_p, sin_p = _rope_tables(jnp.arange(t))
    past_len = page_table.shape[1] * cache_ckv.shape[2]
    cos_s, sin_s = _rope_tables(past_len + jnp.arange(1))
    cache_krope_t = jnp.swapaxes(cache_krope, 2, 3)
    u_all = peer_u.astype(BF16)
    ne = peer_v.shape[1]
    vt_all = jnp.swapaxes(peer_v.reshape(depth, ne // PEER_TE, PEER_TE, d), 2, 3).astype(BF16)

    st_p = {"tm": _tile(t, 512), "tm_in": _tile(t, 1024), "tb": _tile(t, 512), "rpg": t, "r": 1}
    st_s = {"tm": n_s, "tm_in": n_s, "tb": n_s, "rpg": n_s, "r": n_s}

    xp = x_prompt.reshape(n_p, d)
    xs = x_sample.reshape(n_s, d)
    ckv_p, kr_p, stt_p, ckv_s, kr_s, stt_s = [], [], [], [], [], []
    for l in range(depth):
        lw = layers[l]
        final = l == depth - 1
        mod_p = mod[l, :bsz].reshape(bsz, 1, 6 * d)
        mod_s = mod[l, bsz:].reshape(1, n_s, 6 * d)

        z = _in_proj(xp, mod_p, lw["g1"], lw["w_in_p"], st_p)
        q, ckv, kr, kvb = _mla_prep(z, lw, cos_p, sin_p, st_p)
        att = _attn_prompt(q, kvb, lw["w_uv"], bsz, t)
        o_r, s_end = _hgrn_prompt(z, lw["lb3"], bsz, t)
        x1, h2, h2t = _merge(att, o_r, z, xp, mod_p, lw, st_p)
        rt = _router(h2, lw, st_p)
        xp = _peer(h2t, rt, x1, mod_p, u_all, vt_all, l, gf, st_p, final)
        ckv_p.append(ckv.reshape(bsz, t, MLA_KV_RANK))
        kr_p.append(kr.reshape(bsz, t, MLA_ROPE))
        stt_p.append(s_end)

        z = _in_proj(xs, mod_s, lw["g1"], lw["w_in_p"], st_s)
        q, ckv, kr, kvb = _mla_prep(z, lw, cos_s, sin_s, st_s)
        lat = _attn_sample(jnp.transpose(q, (1, 0, 2)), kvb, cache_ckv, cache_krope_t, page_table, l)
        att = _uv_proj(jnp.transpose(lat, (1, 0, 2)), lw["w_uv"])
        o_r, s_new = _hgrn_sample(z, lw["lb3"], state_hgrn, l)
        x1, h2, h2t = _merge(att, o_r, z, xs, mod_s, lw, st_s)
        rt = _router(h2, lw, st_s)
        xs = _peer(h2t, rt, x1, mod_s, u_all, vt_all, l, gf, st_s, final)
        ckv_s.append(ckv.reshape(n_s, 1, MLA_KV_RANK))
        kr_s.append(kr.reshape(n_s, 1, MLA_ROPE))
        stt_s.append(s_new)

    return (xp.reshape(bsz, t, d), xs.reshape(n_s, 1, d),
            jnp.stack(ckv_p), jnp.stack(kr_p), jnp.stack(stt_p),
            jnp.stack(ckv_s), jnp.stack(kr_s), jnp.stack(stt_s))
```

```python
import functools

import jax
import jax.numpy as jnp
from jax import lax
from jax.experimental import pallas as pl
from jax.experimental.pallas import tpu as pltpu

F32 = jnp.float32
BF16 = jnp.bfloat16

MLA_HEADS = 8
MLA_Q_RANK = 512
MLA_KV_RANK = 256
MLA_NOPE = 128
MLA_ROPE = 64
MLA_V = 128
MLA_SCALE = (MLA_NOPE + MLA_ROPE) ** -0.5
ROPE_THETA = 10000.0
HG_HEADS = 8
HG_K = 128
HG_V = 128
PEER_HEADS = 8
PEER_NKEYS = 128
PEER_QDIM = 256
PEER_TOPK = 16
EPS = 1e-6

QK_PAD = 384
HG_CHUNK = 64
HG_SUB = 16
VMEM_LIMIT = 56 * 1024 * 1024
ATT_TQ, ATT_TK, ATT_GP = 256, 512, 64
ATT_HB = 4
LANE, SUBLANE = 128, 8
PEER_TE = 1024
PEER_MM_LANES = 256

COL_GA, COL_GB, COL_A, COL_HQ, COL_HF, COL_HI, COL_HG, IN_PAD = 0, 2048, 4096, 5120, 6144, 7168, 8192, 9216


def _cparams(sem, flags=None):
    return pltpu.CompilerParams(dimension_semantics=sem, vmem_limit_bytes=VMEM_LIMIT, flags=flags)


def _tile(n, pref):
    t = min(n, pref)
    while n % t:
        t -= 8
    return t


def _dot(a, b):
    return jnp.dot(a, b, preferred_element_type=F32)


def _dot_nt(a, b):
    return lax.dot_general(a, b, (((1,), (1,)), ((), ())), preferred_element_type=F32)


def _dot_tn(a, b):
    return lax.dot_general(a, b, (((0,), (0,)), ((), ())), preferred_element_type=F32)


def _split3(x):
    hi = x.astype(BF16)
    r = x - hi.astype(F32)
    mid = r.astype(BF16)
    lo = (r - mid.astype(F32)).astype(BF16)
    return hi, mid, lo


def _rms(x, g):
    return x * lax.rsqrt(jnp.mean(x * x, axis=-1, keepdims=True) + EPS) * g


def _const_spec(shape):
    nd = len(shape)
    return pl.BlockSpec(shape, lambda *_: (0,) * nd)


def _ada_kernel(c_ref, w_ref, b_ref, o_ref):
    c = c_ref[...]
    cs = c * jax.nn.sigmoid(c)
    a_hi, a_mid, _ = _split3(cs)
    w_hi, w_mid, _ = _split3(w_ref[0])
    o_ref[0] = _dot(a_hi, w_hi) + _dot(a_mid, w_hi) + _dot(a_hi, w_mid) + b_ref[0]


def _ada_mod(c_all, w_ada, b_ada):
    depth, d, w6 = w_ada.shape
    r = c_all.shape[0]
    tn = 1024
    return pl.pallas_call(
        _ada_kernel,
        grid=(depth, w6 // tn),
        in_specs=[
            pl.BlockSpec((r, d), lambda l, j: (0, 0)),
            pl.BlockSpec((1, d, tn), lambda l, j: (l, 0, j)),
            pl.BlockSpec((1, 1, tn), lambda l, j: (l, 0, j)),
        ],
        out_specs=pl.BlockSpec((1, r, tn), lambda l, j: (l, 0, j)),
        out_shape=jax.ShapeDtypeStruct((depth, r, w6), F32),
        compiler_params=_cparams(("parallel", "parallel")),
        name="ada_mod",
    )(c_all, w_ada, b_ada.reshape(depth, 1, w6))


def _in_proj_kernel(x_ref, sc_ref, sh_ref, g_ref, w_ref, o_ref, h_scr):
    @pl.when(pl.program_id(1) == 0)
    def _():
        h = _rms(x_ref[...], g_ref[...]) * (1.0 + sc_ref[0]) + sh_ref[0]
        h_scr[...] = h.astype(BF16)

    o_ref[...] = _dot(h_scr[...], w_ref[...])


def _in_proj(x, mod3, g1, w_in_p, st):
    n, d = x.shape
    tm, rpg, r = st["tm_in"], st["rpg"], st["r"]
    tn = 1024
    width = w_in_p.shape[1]

    def gi(i):
        return (i * tm) // rpg

    return pl.pallas_call(
        _in_proj_kernel,
        grid=(n // tm, width // tn),
        in_specs=[
            pl.BlockSpec((tm, d), lambda i, j: (i, 0)),
            pl.BlockSpec((1, r, d), lambda i, j: (gi(i), 0, 1)),
            pl.BlockSpec((1, r, d), lambda i, j: (gi(i), 0, 0)),
            pl.BlockSpec((1, d), lambda i, j: (0, 0)),
            pl.BlockSpec((d, tn), lambda i, j: (0, j)),
        ],
        out_specs=pl.BlockSpec((tm, tn), lambda i, j: (i, j)),
        out_shape=jax.ShapeDtypeStruct((n, width), F32),
        scratch_shapes=[pltpu.VMEM((tm, d), BF16)],
        compiler_params=_cparams(("parallel", "arbitrary")),
        name="in_proj",
    )(x, mod3, mod3, g1, w_in_p)


def _mla_prep_kernel(z_ref, gq_ref, gkv_ref, wn_ref, wr_ref, wrs_ref, wuk_ref, cos_ref, sin_ref,
                     q_ref, ckv_ref, kr_ref, kvb_ref):
    z = z_ref[...]
    cq = z[:, :MLA_Q_RANK]
    ckv = z[:, MLA_Q_RANK:MLA_Q_RANK + MLA_KV_RANK]
    o = MLA_Q_RANK + MLA_KV_RANK
    kr = z[:, o:o + MLA_ROPE]
    krs = z[:, o + MLA_ROPE:o + 2 * MLA_ROPE]
    cos = cos_ref[...]
    sin = sin_ref[...]

    cqn = _rms(cq, gq_ref[...]).astype(BF16)
    qn = _dot(cqn, wn_ref[...])
    qrope = (_dot(cqn, wr_ref[...]) * cos + _dot(cqn, wrs_ref[...]) * sin) * MLA_SCALE
    tm = z.shape[0]
    zpad = jnp.zeros((tm, QK_PAD - MLA_KV_RANK - MLA_ROPE), BF16)
    for h in range(MLA_HEADS):
        ql = _dot(qn[:, h * MLA_NOPE:(h + 1) * MLA_NOPE].astype(BF16), wuk_ref[h]) * MLA_SCALE
        q_ref[h, :, 0:MLA_KV_RANK] = ql.astype(BF16)
        q_ref[h, :, MLA_KV_RANK:MLA_KV_RANK + MLA_ROPE] = qrope[:, h * MLA_ROPE:(h + 1) * MLA_ROPE].astype(BF16)
        q_ref[h, :, MLA_KV_RANK + MLA_ROPE:QK_PAD] = zpad

    ckvn = _rms(ckv, gkv_ref[...])
    kro = kr * cos[:, :MLA_ROPE] + krs * sin[:, :MLA_ROPE]
    ckv_ref[...] = ckvn
    kr_ref[...] = kro
    kvb_ref[:, 0:MLA_KV_RANK] = ckvn.astype(BF16)
    kvb_ref[:, MLA_KV_RANK:MLA_KV_RANK + MLA_ROPE] = kro.astype(BF16)
    kvb_ref[:, MLA_KV_RANK + MLA_ROPE:QK_PAD] = zpad


def _mla_prep(z, lw, cos_t, sin_t, st):
    n = z.shape[0]
    tm = st["tm"]
    rt = cos_t.shape[0]
    hr = MLA_HEADS * MLA_ROPE
    if rt == 1:
        tspec = pl.BlockSpec((1, hr), lambda i: (0, 0))
    else:
        nt = rt // tm
        tspec = pl.BlockSpec((tm, hr), lambda i: (i % nt, 0))
    return pl.pallas_call(
        _mla_prep_kernel,
        grid=(n // tm,),
        in_specs=[
            pl.BlockSpec((tm, 1024), lambda i: (i, COL_A // 1024)),
            _const_spec((1, MLA_Q_RANK)),
            _const_spec((1, MLA_KV_RANK)),
            _const_spec((MLA_Q_RANK, MLA_HEADS * MLA_NOPE)),
            _const_spec((MLA_Q_RANK, hr)),
            _const_spec((MLA_Q_RANK, hr)),
            _const_spec((MLA_HEADS, MLA_NOPE, MLA_KV_RANK)),
            tspec,
            tspec,
        ],
        out_specs=[
            pl.BlockSpec((MLA_HEADS, tm, QK_PAD), lambda i: (0, i, 0)),
            pl.BlockSpec((tm, MLA_KV_RANK), lambda i: (i, 0)),
            pl.BlockSpec((tm, MLA_ROPE), lambda i: (i, 0)),
            pl.BlockSpec((tm, QK_PAD), lambda i: (i, 0)),
        ],
        out_shape=[
            jax.ShapeDtypeStruct((MLA_HEADS, n, QK_PAD), BF16),
            jax.ShapeDtypeStruct((n, MLA_KV_RANK), F32),
            jax.ShapeDtypeStruct((n, MLA_ROPE), F32),
            jax.ShapeDtypeStruct((n, QK_PAD), BF16),
        ],
        compiler_params=_cparams(("parallel",)),
        name="mla_prep",
    )(z, lw["g_q"], lw["g_kv"], lw["w_nope"], lw["w_rope"], lw["w_rope_sw"], lw["w_ukT"], cos_t, sin_t)


def _attn_prompt_kernel(q_ref, k_ref, wuv_ref, o_ref, m_scr, l_scr, acc_scr, *, tq, tk):
    i = pl.program_id(1)
    j = pl.program_id(2)
    nh = MLA_HEADS

    @pl.when(j == 0)
    def _():
        m_scr[...] = jnp.full(m_scr.shape, -jnp.inf, F32)
        l_scr[...] = jnp.zeros(l_scr.shape, F32)
        acc_scr[...] = jnp.zeros(acc_scr.shape, F32)

    def block(masked):
        hb = ATT_HB
        k = k_ref[...]
        v = k[:, :MLA_KV_RANK]
        if masked:
            qpos = i * tq + lax.broadcasted_iota(jnp.int32, (tq, tk), 0)
            kpos = j * tk + lax.broadcasted_iota(jnp.int32, (tq, tk), 1)
            keep = jnp.concatenate([kpos <= qpos] * hb, axis=0)
        for h in range(0, nh, hb):
            hs = slice(h, h + hb)
            s = _dot_nt(q_ref[hs].reshape(hb * tq, QK_PAD), k)
            if masked:
                s = jnp.where(keep, s, -jnp.inf)
            m_prev = m_scr[hs].reshape(hb * tq, 1)
            m_new = jnp.maximum(m_prev, jnp.max(s, axis=-1, keepdims=True))
            corr = jnp.exp(m_prev - m_new)
            p = jnp.exp(s - m_new)
            l_new = l_scr[hs].reshape(hb * tq, 1) * corr + jnp.sum(p, axis=-1, keepdims=True)
            a_new = acc_scr[hs].reshape(hb * tq, MLA_KV_RANK) * corr + _dot(p.astype(BF16), v)
            l_scr[hs] = l_new.reshape(hb, tq, 1)
            acc_scr[hs] = a_new.reshape(hb, tq, MLA_KV_RANK)
            m_scr[hs] = m_new.reshape(hb, tq, 1)

    @pl.when(j * tk + tk - 1 <= i * tq)
    def _():
        block(False)

    @pl.when(jnp.logical_and(j * tk + tk - 1 > i * tq, j * tk <= i * tq + tq - 1))
    def _():
        block(True)

    @pl.when(j == pl.num_programs(2) - 1)
    def _():
        for h in range(nh):
            lat = acc_scr[h] / l_scr[h]
            o_ref[:, h * MLA_V:(h + 1) * MLA_V] = _dot(lat.astype(BF16), wuv_ref[h]).astype(BF16)


def _attn_prompt(q, kvb, wuv, bsz, t):
    n = kvb.shape[0]
    tq = _tile(t, ATT_TQ)
    tk = _tile(t, ATT_TK)
    nq, nk = t // tq, t // tk
    kern = functools.partial(_attn_prompt_kernel, tq=tq, tk=tk)
    return pl.pallas_call(
        kern,
        grid=(bsz, nq, nk),
        in_specs=[
            pl.BlockSpec((MLA_HEADS, tq, QK_PAD), lambda b, i, j: (0, b * nq + i, 0)),
            pl.BlockSpec((tk, QK_PAD), lambda b, i, j: (b * nk + jnp.minimum(j, (i * tq + tq - 1) // tk), 0)),
            _const_spec((MLA_HEADS, MLA_KV_RANK, MLA_V)),
        ],
        out_specs=pl.BlockSpec((tq, MLA_HEADS * MLA_V), lambda b, i, j: (b * nq + i, 0)),
        out_shape=jax.ShapeDtypeStruct((n, MLA_HEADS * MLA_V), BF16),
        scratch_shapes=[
            pltpu.VMEM((MLA_HEADS, tq, 1), F32),
            pltpu.VMEM((MLA_HEADS, tq, 1), F32),
            pltpu.VMEM((MLA_HEADS, tq, MLA_KV_RANK), F32),
        ],
        compiler_params=_cparams(("parallel", "parallel", "arbitrary")),
        name="attn_prompt",
    )(q, kvb, wuv)


def _attn_sample_kernel(pt_ref, q_ref, knew_ref, ckv_hbm, kr_hbm, o_ref,
                        kbuf, rbuf, sem, m_scr, l_scr, acc_scr, *, layer, gp, ng, npg, nsteps):
    b = pl.program_id(0)
    g = pl.program_id(1)
    step = b * ng + g
    slot = step % 2

    def copies(st, sl):
        base = (st // ng) * npg + (st % ng) * gp
        cps = []
        for k in range(gp):
            page = pt_ref[base + k]
            cps.append(pltpu.make_async_copy(ckv_hbm.at[layer, page], kbuf.at[sl, k], sem.at[0, sl]))
            cps.append(pltpu.make_async_copy(kr_hbm.at[layer, page], rbuf.at[sl, k], sem.at[1, sl]))
        return cps

    @pl.when(step == 0)
    def _():
        for c in copies(step, slot):
            c.start()

    @pl.when(step + 1 < nsteps)
    def _():
        for c in copies(step + 1, 1 - slot):
            c.start()

    for c in copies(step, slot):
        c.wait()

    @pl.when(g == 0)
    def _():
        m_scr[...] = jnp.full(m_scr.shape, -jnp.inf, F32)
        l_scr[...] = jnp.zeros(l_scr.shape, F32)
        acc_scr[...] = jnp.zeros(acc_scr.shape, F32)

    q = q_ref[0]
    ps = kbuf.shape[2]
    kc = kbuf[slot].reshape(gp * ps, MLA_KV_RANK).astype(BF16)
    qr = q[:, MLA_KV_RANK:MLA_KV_RANK + MLA_ROPE]
    s_rope = jnp.concatenate([_dot(qr, rbuf[slot, k].astype(BF16)) for k in range(gp)], axis=1)
    s = _dot_nt(q[:, :MLA_KV_RANK], kc) + s_rope
    m_prev = m_scr[...]
    m_new = jnp.maximum(m_prev, jnp.max(s, axis=-1, keepdims=True))
    corr = jnp.exp(m_prev - m_new)
    p = jnp.exp(s - m_new)
    l_scr[...] = l_scr[...] * corr + jnp.sum(p, axis=-1, keepdims=True)
    acc_scr[...] = acc_scr[...] * corr + _dot(p.astype(BF16), kc)
    m_scr[...] = m_new

    @pl.when(g == ng - 1)
    def _():
        knew = knew_ref[0].astype(F32)
        s_new = jnp.sum(q.astype(F32) * knew, axis=-1, keepdims=True)
        m_old = m_scr[...]
        m_fin = jnp.maximum(m_old, s_new)
        c_old = jnp.exp(m_old - m_fin)
        p_new = jnp.exp(s_new - m_fin)
        l_fin = l_scr[...] * c_old + p_new
        acc = acc_scr[...] * c_old + p_new * knew[:, :MLA_KV_RANK]
        o_ref[0] = acc / l_fin


def _attn_sample(q_nh, kvb, cache_ckv, cache_krope_t, page_table, layer):
    n = q_nh.shape[0]
    npg = page_table.shape[1]
    ps = cache_ckv.shape[2]
    gp = min(npg, ATT_GP)
    while npg % gp:
        gp -= 1
    ng = npg // gp
    kern = functools.partial(_attn_sample_kernel, layer=layer, gp=gp, ng=ng, npg=npg, nsteps=n * ng)
    grid_spec = pltpu.PrefetchScalarGridSpec(
        num_scalar_prefetch=1,
        grid=(n, ng),
        in_specs=[
            pl.BlockSpec((1, MLA_HEADS, QK_PAD), lambda b, g, pt: (b, 0, 0)),
            pl.BlockSpec((1, 1, QK_PAD), lambda b, g, pt: (b, 0, 0)),
            pl.BlockSpec(memory_space=pl.ANY),
            pl.BlockSpec(memory_space=pl.ANY),
        ],
        out_specs=pl.BlockSpec((1, MLA_HEADS, MLA_KV_RANK), lambda b, g, pt: (b, 0, 0)),
        scratch_shapes=[
            pltpu.VMEM((2, gp, ps, MLA_KV_RANK), F32),
            pltpu.VMEM((2, gp, MLA_ROPE, ps), F32),
            pltpu.SemaphoreType.DMA((2, 2)),
            pltpu.VMEM((MLA_HEADS, 1), F32),
            pltpu.VMEM((MLA_HEADS, 1), F32),
            pltpu.VMEM((MLA_HEADS, MLA_KV_RANK), F32),
        ],
    )
    return pl.pallas_call(
        kern,
        grid_spec=grid_spec,
        out_shape=jax.ShapeDtypeStruct((n, MLA_HEADS, MLA_KV_RANK), F32),
        compiler_params=_cparams(("arbitrary", "arbitrary")),
        name="attn_sample",
    )(page_table.reshape(-1), q_nh, kvb.reshape(n, 1, QK_PAD), cache_ckv, cache_krope_t)


def _uv_proj_kernel(lat_ref, wuv_ref, o_ref):
    for h in range(MLA_HEADS):
        o_ref[:, h * MLA_V:(h + 1) * MLA_V] = _dot(lat_ref[h].astype(BF16), wuv_ref[h]).astype(BF16)


def _uv_proj(lat_hn, wuv):
    n = lat_hn.shape[1]
    return pl.pallas_call(
        _uv_proj_kernel,
        grid=(1,),
        in_specs=[_const_spec(lat_hn.shape), _const_spec(wuv.shape)],
        out_specs=_const_spec((n, MLA_HEADS * MLA_V)),
        out_shape=jax.ShapeDtypeStruct((n, MLA_HEADS * MLA_V), BF16),
        compiler_params=_cparams(("arbitrary",)),
        name="uv_proj",
    )(lat_hn, wuv)


def _hg_gates(fp, lb_ref):
    lsig = jnp.minimum(fp, 0.0) - jnp.log1p(jnp.exp(-jnp.abs(fp)))
    bterm = lb_ref[1:2, :] + lsig
    loglb = lb_ref[0:1, :]
    logf = jnp.maximum(loglb, bterm) + jnp.log1p(jnp.exp(-jnp.abs(loglb - bterm)))
    kk = lb_ref[2:3, :] * jax.nn.sigmoid(-fp)
    return logf, kk


def _hgrn_prompt_kernel(q_ref, f_ref, i_ref, lb_ref, o_ref, s_ref, st_scr, *, c, sb):
    ci = pl.program_id(1)
    nsb = c // sb
    kd = HG_K

    @pl.when(ci == 0)
    def _():
        st_scr[...] = jnp.zeros(st_scr.shape, F32)

    logf, kk_all = _hg_gates(f_ref[...], lb_ref)
    tril = (lax.broadcasted_iota(jnp.int32, (c, c), 0) >= lax.broadcasted_iota(jnp.int32, (c, c), 1)).astype(BF16)
    l_hi, l_mid, l_lo = _split3(logf)
    cum_all = _dot(tril, l_hi) + _dot(tril, l_mid) + _dot(tril, l_lo)
    q_all = q_ref[...]
    v_all = i_ref[...]
    ones = jnp.ones((kd, kd), BF16)
    tpos = lax.broadcasted_iota(jnp.int32, (nsb, sb, kd), 1)

    for h in range(HG_HEADS):
        hs = slice(h * kd, (h + 1) * kd)
        q = q_all[:, hs]
        k = kk_all[:, hs]
        v = v_all[:, hs]
        cu = cum_all[:, hs]
        s0 = st_scr[h]
        vb = v.astype(BF16)

        o_inter = _dot((q * jnp.exp(cu)).astype(BF16), s0.astype(BF16))
        blocks = [o_inter[i * sb:(i + 1) * sb] for i in range(nsb)]

        for j in range(nsb - 1):
            r0 = (j + 1) * sb
            bj = cu[r0 - 1:r0, :]
            ke = (k[j * sb:r0] * jnp.exp(bj - cu[j * sb:r0])).astype(BF16)
            qe = (q[r0:] * jnp.exp(cu[r0:] - bj)).astype(BF16)
            a = _dot_nt(qe, ke)
            lower = _dot(a.astype(BF16), vb[j * sb:r0])
            for i in range(j + 1, nsb):
                blocks[i] = blocks[i] + lower[(i - j - 1) * sb:(i - j) * sb]

        q4 = q.reshape(nsb, sb, kd)
        k4 = k.reshape(nsb, sb, kd)
        v4 = v.reshape(nsb, sb, kd)
        cu4 = cu.reshape(nsb, sb, kd)
        od = jnp.zeros((nsb, sb, kd), F32)
        for s in range(sb):
            d = cu4 - cu4[:, s:s + 1, :]
            w = jnp.where(tpos >= s, jnp.exp(d), 0.0)
            zz = (q4 * w * k4[:, s:s + 1, :]).reshape(c, kd).astype(BF16)
            r = _dot(zz, ones).reshape(nsb, sb, kd)
            od = od + r * v4[:, s:s + 1, :]

        o_ref[:, hs] = jnp.concatenate(blocks, axis=0) + od.reshape(c, kd)

        c_end = cu[c - 1:c, :]
        kdec = (k * jnp.exp(c_end - cu)).astype(BF16)
        upd = _dot_tn(kdec, vb)
        e_col = jnp.broadcast_to(jnp.exp(c_end), (kd, kd)).T
        st_scr[h] = e_col * s0 + upd

    @pl.when(ci == pl.num_programs(1) - 1)
    def _():
        s_ref[0] = st_scr[...]


def _hgrn_prompt(z, lb3, bsz, t):
    n = z.shape[0]
    c = _tile(t, HG_CHUNK)
    sb = min(HG_SUB, c)
    nc = t // c
    w = HG_HEADS * HG_K
    kern = functools.partial(_hgrn_prompt_kernel, c=c, sb=sb)
    return pl.pallas_call(
        kern,
        grid=(bsz, nc),
        in_specs=[
            pl.BlockSpec((c, w), lambda b, ci: (b * nc + ci, COL_HQ // w)),
            pl.BlockSpec((c, w), lambda b, ci: (b * nc + ci, COL_HF // w)),
            pl.BlockSpec((c, w), lambda b, ci: (b * nc + ci, COL_HI // w)),
            _const_spec((3, w)),
        ],
        out_specs=[
            pl.BlockSpec((c, w), lambda b, ci: (b * nc + ci, 0)),
            pl.BlockSpec((1, HG_HEADS, HG_K, HG_V), lambda b, ci: (b, 0, 0, 0)),
        ],
        out_shape=[
            jax.ShapeDtypeStruct((n, w), F32),
            jax.ShapeDtypeStruct((bsz, HG_HEADS, HG_K, HG_V), F32),
        ],
        scratch_shapes=[pltpu.VMEM((HG_HEADS, HG_K, HG_V), F32)],
        compiler_params=_cparams(("parallel", "arbitrary")),
        name="hgrn_prompt",
    )(z, z, z, lb3)


def _hgrn_sample_kernel(q_ref, f_ref, i_ref, lb_ref, s0_ref, o_ref, s_ref, *, tb):
    kd = HG_K
    logf, kk_all = _hg_gates(f_ref[...], lb_ref)
    f_all = jnp.exp(logf)
    q_all = q_ref[...]
    v_all = i_ref[...]
    for b in range(tb):
        for h in range(HG_HEADS):
            hs = slice(h * kd, (h + 1) * kd)
            e_f = jnp.broadcast_to(f_all[b:b + 1, hs], (kd, kd)).T
            e_k = jnp.broadcast_to(kk_all[b:b + 1, hs], (kd, kd)).T
            s_new = e_f * s0_ref[b, h] + e_k * v_all[b:b + 1, hs]
            s_ref[b, h] = s_new
            qrow = jnp.broadcast_to(q_all[b:b + 1, hs], (8, kd)).astype(BF16)
            o_ref[b:b + 1, hs] = _dot(qrow, s_new.astype(BF16))[0:1]


def _hgrn_sample(z, lb3, state, layer):
    n = z.shape[0]
    tb = 8
    w = HG_HEADS * HG_K
    kern = functools.partial(_hgrn_sample_kernel, tb=tb)
    return pl.pallas_call(
        kern,
        grid=(n // tb,),
        in_specs=[
            pl.BlockSpec((tb, w), lambda i: (i, COL_HQ // w)),
            pl.BlockSpec((tb, w), lambda i: (i, COL_HF // w)),
            pl.BlockSpec((tb, w), lambda i: (i, COL_HI // w)),
            _const_spec((3, w)),
            pl.BlockSpec((None, tb, HG_HEADS, HG_K, HG_V), lambda i: (layer, i, 0, 0, 0)),
        ],
        out_specs=[
            pl.BlockSpec((tb, w), lambda i: (i, 0)),
            pl.BlockSpec((tb, HG_HEADS, HG_K, HG_V), lambda i: (i, 0, 0, 0)),
        ],
        out_shape=[
            jax.ShapeDtypeStruct((n, w), F32),
            jax.ShapeDtypeStruct(state.shape[1:], F32),
        ],
        compiler_params=_cparams(("parallel",)),
        name="hgrn_sample",
    )(z, z, z, lb3, state)


def _merge_kernel(att_ref, or_ref, hg_ref, ga_ref, gb_ref, x_ref, gt1_ref, sc2_ref, sh2_ref,
                  ghg_ref, g2_ref, wa_ref, wb_ref, wo_ref, x1_ref, h2_ref, h2t_ref):
    o_r = or_ref[...]
    hg = hg_ref[...]
    ghg = ghg_ref[...]
    parts = []
    for h in range(HG_HEADS):
        hs = slice(h * HG_V, (h + 1) * HG_V)
        g = hg[:, hs]
        parts.append((_rms(o_r[:, hs], ghg) * (g * jax.nn.sigmoid(g))).astype(BF16))
    orn = jnp.concatenate(parts, axis=-1)
    oa = _dot(att_ref[...], wa_ref[...])
    ob = _dot(orn, wb_ref[...])
    m = jax.nn.sigmoid(ga_ref[...]) * oa + jax.nn.sigmoid(gb_ref[...]) * ob
    y = _dot(m.astype(BF16), wo_ref[...])
    x1 = x_ref[...] + gt1_ref[0] * y
    x1_ref[...] = x1
    h2 = _rms(x1, g2_ref[...]) * (1.0 + sc2_ref[0]) + sh2_ref[0]
    h2_ref[...] = h2.astype(BF16)
    h2t_ref[...] = h2.T.astype(BF16)


def _merge(att, o_r, z, x, mod3, lw, st):
    n, d = x.shape
    tm, rpg, r = min(st["tm"], 256), st["rpg"], st["r"]
    w = HG_HEADS * HG_V

    def gi(i):
        return (i * tm) // rpg

    def mspec(k):
        return pl.BlockSpec((1, r, d), lambda i: (gi(i), 0, k))

    one = pl.Buffered(1)
    return pl.pallas_call(
        _merge_kernel,
        grid=(n // tm,),
        in_specs=[
            pl.BlockSpec((tm, w), lambda i: (i, 0)),
            pl.BlockSpec((tm, w), lambda i: (i, 0)),
            pl.BlockSpec((tm, w), lambda i: (i, COL_HG // w)),
            pl.BlockSpec((tm, d), lambda i: (i, COL_GA // d)),
            pl.BlockSpec((tm, d), lambda i: (i, COL_GB // d)),
            pl.BlockSpec((tm, d), lambda i: (i, 0)),
            mspec(2), mspec(4), mspec(3),
            _const_spec((1, HG_V)),
            _const_spec((1, d)),
            pl.BlockSpec((w, d), lambda i: (0, 0), pipeline_mode=one),
            pl.BlockSpec((w, d), lambda i: (0, 0), pipeline_mode=one),
            pl.BlockSpec((d, d), lambda i: (0, 0), pipeline_mode=one),
        ],
        out_specs=[
            pl.BlockSpec((tm, d), lambda i: (i, 0)),
            pl.BlockSpec((tm, d), lambda i: (i, 0)),
            pl.BlockSpec((d, tm), lambda i: (0, i)),
        ],
        out_shape=[
            jax.ShapeDtypeStruct((n, d), F32),
            jax.ShapeDtypeStruct((n, d), BF16),
            jax.ShapeDtypeStruct((d, n), BF16),
        ],
        compiler_params=_cparams(("parallel",)),
        name="merge",
    )(att, o_r, z, z, z, x, mod3, mod3, mod3, lw["g_hg"], lw["g2"], lw["w_a"], lw["w_b"], lw["w_out"])


def _top_sorted(s, k):
    vals = []
    for _ in range(k):
        m = jnp.max(s, axis=0, keepdims=True)
        vals.append(m)
        s = jnp.where(s == m, -jnp.inf, s)
    return vals


def _router_kernel(h2_ref, wq_ref, k1_ref, k2_ref, thr_ref, a_ref, s2_ref, e2_ref, v2_scr):
    qp = _dot(h2_ref[...], wq_ref[...])
    half = PEER_QDIM // 2
    k1 = k1_ref[...]
    k2 = k2_ref[...]
    topk = PEER_TOPK
    nx = topk + 1
    for h in range(PEER_HEADS):
        qa = qp[:, h * PEER_QDIM:h * PEER_QDIM + half].astype(BF16)
        qb = qp[:, h * PEER_QDIM + half:(h + 1) * PEER_QDIM].astype(BF16)
        s1 = _dot_nt(k1, qa)
        s2 = _dot_nt(k2, qb)
        v1 = _top_sorted(s1, nx)
        v2 = _top_sorted(s2, nx)
        v2_scr[...] = jnp.full(v2_scr.shape, -jnp.inf, F32)
        for a in range(nx):
            v2_scr[a:a + 1, :] = v2[a]
        pieces = []
        for a in range(nx):
            nb = -(-(nx // (a + 1)) // 8) * 8
            pieces.append(v1[a] + v2_scr[0:nb, :])
        cand = jnp.concatenate(pieces, axis=0)
        sc = _top_sorted(cand, nx)
        mid = 0.5 * (sc[topk - 1] + sc[topk])
        zsum = jnp.zeros_like(mid)
        for a in range(topk):
            zsum = zsum + jnp.exp(sc[a] - sc[0])
        thr = mid - s1
        aw = jnp.exp(s1 - v1[0]) / zsum
        e2 = jnp.exp(s2 - v2[0])
        for j in range(thr_ref.shape[0]):
            js = slice(j * LANE, (j + 1) * LANE)
            thr_ref[j, h] = thr[:, js]
            a_ref[j, h] = aw[:, js]
            s2_ref[j, h] = s2[:, js]
            e2_ref[j, h] = e2[:, js]


def _router(h2, lw, st):
    n, d = h2.shape
    tm = min(st["tm"], 256)
    nk = PEER_NKEYS
    half = PEER_QDIM // 2
    oshape = (n // LANE, PEER_HEADS, nk, LANE)
    ospec = pl.BlockSpec((tm // LANE, PEER_HEADS, nk, LANE), lambda i: (i, 0, 0, 0))
    return pl.pallas_call(
        _router_kernel,
        grid=(n // tm,),
        in_specs=[
            pl.BlockSpec((tm, d), lambda i: (i, 0)),
            pl.BlockSpec((d, PEER_HEADS * PEER_QDIM), lambda i: (0, 0), pipeline_mode=pl.Buffered(1)),
            _const_spec((nk, half)),
            _const_spec((nk, half)),
        ],
        out_specs=[ospec, ospec, ospec, ospec],
        out_shape=[jax.ShapeDtypeStruct(oshape, F32)] * 4,
        scratch_shapes=[pltpu.VMEM((-(-(PEER_TOPK + 1) // 8) * 8, tm), F32)],
        compiler_params=_cparams(("parallel",)),
        name="peer_router",
    )(h2, lw["peer_wq"], lw["k1"], lw["k2"])


def _peer_kernel(h2t_ref, u_ref, vt_ref, thr_ref, a_ref, s2_ref, e2_ref, x1_ref, gt2_ref, gf_ref,
                 o_ref, acc_scr, pre_scr, p_scr, *, te, final):
    e = pl.program_id(1)
    ni = te // PEER_NKEYS
    nk = PEER_NKEYS
    tb = h2t_ref.shape[1]

    @pl.when(e == 0)
    def _():
        acc_scr[...] = jnp.zeros(acc_scr.shape, F32)

    pre_scr[...] = _dot(u_ref[...], h2t_ref[...])
    for t in range(tb // LANE):
        ls = slice(t * LANE, (t + 1) * LANE)
        for ii in range(ni):
            w = None
            for h in range(PEER_HEADS):
                thr = thr_ref[t, h, pl.ds(e * ni + ii, 1), :]
                arow = a_ref[t, h, pl.ds(e * ni + ii, 1), :]
                wh = jnp.where(s2_ref[t, h] >= thr, arow * e2_ref[t, h], 0.0)
                w = wh if w is None else w + wh
            pre = pre_scr[ii * nk:(ii + 1) * nk, ls]
            act = 0.5 * pre * (1.0 + lax.erf(pre * (2.0 ** -0.5)))
            p_scr[ii * nk:(ii + 1) * nk, ls] = (w * act).astype(BF16)
    acc_scr[...] += _dot(vt_ref[...], p_scr[...])

    @pl.when(e == pl.num_programs(1) - 1)
    def _():
        x2 = x1_ref[...] + gt2_ref[0] * acc_scr[...].T
        o_ref[...] = _rms(x2, gf_ref[...]) if final else x2


def _peer(h2t, rt, x1, mod3, u_all, vt_all, layer, g_final, st, final):
    n, d = x1.shape
    tb, rpg, r = st["tb"], st["rpg"], st["r"]
    ne = u_all.shape[1]
    te = PEER_TE
    nk = PEER_NKEYS

    def gi(i):
        return (i * tb) // rpg

    one = pl.Buffered(1)
    sspec = pl.BlockSpec((tb // LANE, PEER_HEADS, nk, LANE), lambda i, e: (i, 0, 0, 0), pipeline_mode=one)
    kern = functools.partial(_peer_kernel, te=te, final=final)
    return pl.pallas_call(
        kern,
        grid=(n // tb, ne // te),
        in_specs=[
            pl.BlockSpec((d, tb), lambda i, e: (0, i), pipeline_mode=one),
            pl.BlockSpec((None, te, d), lambda i, e: (layer, e, 0)),
            pl.BlockSpec((None, None, d, te), lambda i, e: (layer, e, 0, 0)),
            sspec, sspec, sspec, sspec,
            pl.BlockSpec((tb, d), lambda i, e: (i, 0), pipeline_mode=one),
            pl.BlockSpec((1, r, d), lambda i, e: (gi(i), 0, 5)),
            pl.BlockSpec((1, d), lambda i, e: (0, 0)),
        ],
        out_specs=pl.BlockSpec((tb, d), lambda i, e: (i, 0)),
        out_shape=jax.ShapeDtypeStruct((n, d), F32),
        scratch_shapes=[pltpu.VMEM((d, tb), F32), pltpu.VMEM((te, tb), F32), pltpu.VMEM((te, tb), BF16)],
        compiler_params=_cparams(("parallel", "arbitrary")),
        name="peer_dense",
    )(h2t, u_all, vt_all, rt[0], rt[1], rt[2], rt[3], x1, mod3, g_final)


def _prep_layer(l, w_in, g_norm1, g_qnorm, w_uq, g_kvnorm, w_uk, w_uv, w_a, lbs, g_hg_onorm, w_b,
                w_out, g_norm2, peer_wq, peer_k1, peer_k2, peer_u, peer_v):
    d = w_in.shape[1]
    wi = w_in[l]
    o_kr = MLA_Q_RANK + MLA_KV_RANK
    o_hq = o_kr + MLA_ROPE
    hw = HG_HEADS * HG_K
    o_ga = o_hq + 4 * hw
    assert (o_hq, hw, o_ga + 2 * d, d) == (832, 1024, 9024, 2048)
    kr = wi[:, o_kr:o_hq]
    half = MLA_ROPE // 2
    kr_sw = jnp.concatenate([kr[:, half:], kr[:, :half]], axis=1)
    a_pad = jnp.zeros((d, COL_HQ - COL_A - o_hq - MLA_ROPE), F32)
    w_in_p = jnp.concatenate(
        [wi[:, o_ga:o_ga + d], wi[:, o_ga + d:o_ga + 2 * d], wi[:, :o_hq], kr_sw, a_pad, wi[:, o_hq:o_ga]],
        axis=1).astype(BF16)
    uq = w_uq[l].reshape(MLA_Q_RANK, MLA_HEADS, MLA_NOPE + MLA_ROPE)
    w_nope = uq[:, :, :MLA_NOPE].reshape(MLA_Q_RANK, MLA_HEADS * MLA_NOPE)
    rope = uq[:, :, MLA_NOPE:]
    rope_sw = jnp.concatenate([rope[..., half:], rope[..., :half]], axis=-1)
    lb = lbs[l].reshape(1, hw)
    lb3 = jnp.concatenate([jnp.log(lb), jnp.log1p(-lb), 1.0 - lb], axis=0)
    return {
        "w_in_p": w_in_p,
        "g1": g_norm1[l].reshape(1, d),
        "g_q": g_qnorm[l].reshape(1, MLA_Q_RANK),
        "g_kv": g_kvnorm[l].reshape(1, MLA_KV_RANK),
        "w_nope": w_nope.astype(BF16),
        "w_rope": rope.reshape(MLA_Q_RANK, MLA_HEADS * MLA_ROPE).astype(BF16),
        "w_rope_sw": rope_sw.reshape(MLA_Q_RANK, MLA_HEADS * MLA_ROPE).astype(BF16),
        "w_ukT": jnp.transpose(w_uk[l], (1, 2, 0)).astype(BF16),
        "w_uv": jnp.transpose(w_uv[l], (1, 0, 2)).astype(BF16),
        "w_a": w_a[l].astype(BF16),
        "w_b": w_b[l].astype(BF16),
        "w_out": w_out[l].astype(BF16),
        "lb3": lb3,
        "g_hg": g_hg_onorm[l].reshape(1, HG_V),
        "g2": g_norm2[l].reshape(1, d),
        "peer_wq": peer_wq[l].astype(BF16),
        "k1": peer_k1[l].astype(BF16),
        "k2": peer_k2[l].astype(BF16),
    }


def _rope_tables(pos):
    half = MLA_ROPE // 2
    freq = ROPE_THETA ** (-jnp.arange(half, dtype=F32) / half)
    ang = pos.astype(F32)[:, None] * freq[None, :]
    cos = jnp.cos(ang)
    sin = jnp.sin(ang)
    cos_t = jnp.tile(jnp.concatenate([cos, cos], axis=1), (1, MLA_HEADS))
    sin_t = jnp.tile(jnp.concatenate([-sin, sin], axis=1), (1, MLA_HEADS))
    return cos_t, sin_t


def kernel(x_prompt, x_sample, cache_ckv, cache_krope, state_hgrn, page_table, c_prompt, c_sample, w_ada, b_ada, g_norm1, w_in, g_qnorm, w_uq, g_kvnorm, w_uk, w_uv, w_a, hg_lb_logits, g_hg_onorm, w_b, w_out, g_norm2, peer_wq, peer_k1, peer_k2, peer_u, peer_v, g_final):
    bsz, t, d = x_prompt.shape
    nsq, ts, _ = x_sample.shape
    assert ts == 1
    depth = w_in.shape[0]
    n_p = bsz * t
    n_s = nsq

    lbs = jnp.cumsum(jax.nn.softmax(hg_lb_logits.astype(F32), axis=0), axis=0)
    lbs = lbs - lbs[0]
    layers = [
        _prep_layer(l, w_in, g_norm1, g_qnorm, w_uq, g_kvnorm, w_uk, w_uv, w_a, lbs, g_hg_onorm, w_b,
                    w_out, g_norm2, peer_wq, peer_k1, peer_k2, peer_u, peer_v)
        for l in range(depth)
    ]
    gf = g_final.reshape(1, d)

    mod = _ada_mod(jnp.concatenate([c_prompt, c_sample], axis=0), w_ada, b_ada)
    cos_p, sin_p = _rope_tables(jnp.arange(t))
    past_len = page_table.shape[1] * cache_ckv.shape[2]
    cos_s, sin_s = _rope_tables(past_len + jnp.arange(1))
    cache_krope_t = jnp.swapaxes(cache_krope, 2, 3)
    u_all = peer_u.astype(BF16)
    ne = peer_v.shape[1]
    vt_all = jnp.swapaxes(peer_v.reshape(depth, ne // PEER_TE, PEER_TE, d), 2, 3).astype(BF16)

    st_p = {"tm": _tile(t, 512), "tm_in": _tile(t, 1024), "tb": _tile(t, 512), "rpg": t, "r": 1}
    st_s = {"tm": n_s, "tm_in": n_s, "tb": n_s, "rpg": n_s, "r": n_s}

    xp = x_prompt.reshape(n_p, d)
    xs = x_sample.reshape(n_s, d)
    ckv_p, kr_p, stt_p, ckv_s, kr_s, stt_s = [], [], [], [], [], []
    for l in range(depth):
        lw = layers[l]
        final = l == depth - 1
        mod_p = mod[l, :bsz].reshape(bsz, 1, 6 * d)
        mod_s = mod[l, bsz:].reshape(1, n_s, 6 * d)

        z = _in_proj(xp, mod_p, lw["g1"], lw["w_in_p"], st_p)
        q, ckv, kr, kvb = _mla_prep(z, lw, cos_p, sin_p, st_p)
        att = _attn_prompt(q, kvb, lw["w_uv"], bsz, t)
        o_r, s_end = _hgrn_prompt(z, lw["lb3"], bsz, t)
        x1, h2, h2t = _merge(att, o_r, z, xp, mod_p, lw, st_p)
        rt = _router(h2, lw, st_p)
        xp = _peer(h2t, rt, x1, mod_p, u_all, vt_all, l, gf, st_p, final)
        ckv_p.append(ckv.reshape(bsz, t, MLA_KV_RANK))
        kr_p.append(kr.reshape(bsz, t, MLA_ROPE))
        stt_p.append(s_end)

        z = _in_proj(xs, mod_s, lw["g1"], lw["w_in_p"], st_s)
        q, ckv, kr, kvb = _mla_prep(z, lw, cos_s, sin_s, st_s)
        lat = _attn_sample(jnp.transpose(q, (1, 0, 2)), kvb, cache_ckv, cache_krope_t, page_table, l)
        att = _uv_proj(jnp.transpose(lat, (1, 0, 2)), lw["w_uv"])
        o_r, s_new = _hgrn_sample(z, lw["lb3"], state_hgrn, l)
        x1, h2, h2t = _merge(att, o_r, z, xs, mod_s, lw, st_s)
        rt = _router(h2, lw, st_s)
        xs = _peer(h2t, rt, x1, mod_s, u_all, vt_all, l, gf, st_s, final)
        ckv_s.append(ckv.reshape(n_s, 1, MLA_KV_RANK))
        kr_s.append(kr.reshape(n_s, 1, MLA_ROPE))
        stt_s.append(s_new)

    return (xp.reshape(bsz, t, d), xs.reshape(n_s, 1, d),
            jnp.stack(ckv_p), jnp.stack(kr_p), jnp.stack(stt_p),
            jnp.stack(ckv_s), jnp.stack(kr_s), jnp.stack(stt_s))
```

```python
import functools

import jax
import jax.numpy as jnp
from jax import lax
from jax.experimental import pallas as pl
from jax.experimental.pallas import tpu as pltpu

F32 = jnp.float32
BF16 = jnp.bfloat16

MLA_HEADS = 8
MLA_Q_RANK = 512
MLA_KV_RANK = 256
MLA_NOPE = 128
MLA_ROPE = 64
MLA_V = 128
MLA_SCALE = (MLA_NOPE + MLA_ROPE) ** -0.5
ROPE_THETA = 10000.0
HG_HEADS = 8
HG_K = 128
HG_V = 128
PEER_HEADS = 8
PEER_NKEYS = 128
PEER_QDIM = 256
PEER_TOPK = 16
EPS = 1e-6

QK_PAD = 384
HG_CHUNK = 64
HG_SUB = 16
VMEM_LIMIT = 56 * 1024 * 1024
ATT_TQ, ATT_TK, ATT_GP = 256, 512, 64
ATT_HB = 4
LANE, SUBLANE = 128, 8
PEER_TE = 1024
PEER_SPLIT = 4
PEER_MM_LANES = 256

COL_GA, COL_GB, COL_A, COL_HQ, COL_HF, COL_HI, COL_HG, IN_PAD = 0, 2048, 4096, 5120, 6144, 7168, 8192, 9216


def _cparams(sem, flags=None):
    return pltpu.CompilerParams(dimension_semantics=sem, vmem_limit_bytes=VMEM_LIMIT, flags=flags)


def _tile(n, pref):
    t = min(n, pref)
    while n % t:
        t -= 8
    return t


def _dot(a, b):
    return jnp.dot(a, b, preferred_element_type=F32)


def _dot_nt(a, b):
    return lax.dot_general(a, b, (((1,), (1,)), ((), ())), preferred_element_type=F32)


def _dot_tn(a, b):
    return lax.dot_general(a, b, (((0,), (0,)), ((), ())), preferred_element_type=F32)


def _split3(x):
    hi = x.astype(BF16)
    r = x - hi.astype(F32)
    mid = r.astype(BF16)
    lo = (r - mid.astype(F32)).astype(BF16)
    return hi, mid, lo


def _rms(x, g):
    return x * lax.rsqrt(jnp.mean(x * x, axis=-1, keepdims=True) + EPS) * g


def _const_spec(shape):
    nd = len(shape)
    return pl.BlockSpec(shape, lambda *_: (0,) * nd)


def _ada_kernel(c_ref, w_ref, b_ref, o_ref):
    c = c_ref[...]
    cs = c * jax.nn.sigmoid(c)
    a_hi, a_mid, _ = _split3(cs)
    w_hi, w_mid, _ = _split3(w_ref[0])
    o_ref[0] = _dot(a_hi, w_hi) + _dot(a_mid, w_hi) + _dot(a_hi, w_mid) + b_ref[0]


def _ada_mod(c_all, w_ada, b_ada):
    depth, d, w6 = w_ada.shape
    r = c_all.shape[0]
    tn = 1024
    return pl.pallas_call(
        _ada_kernel,
        grid=(depth, w6 // tn),
        in_specs=[
            pl.BlockSpec((r, d), lambda l, j: (0, 0)),
            pl.BlockSpec((1, d, tn), lambda l, j: (l, 0, j)),
            pl.BlockSpec((1, 1, tn), lambda l, j: (l, 0, j)),
        ],
        out_specs=pl.BlockSpec((1, r, tn), lambda l, j: (l, 0, j)),
        out_shape=jax.ShapeDtypeStruct((depth, r, w6), F32),
        compiler_params=_cparams(("parallel", "parallel")),
        name="ada_mod",
    )(c_all, w_ada, b_ada.reshape(depth, 1, w6))


def _in_proj_kernel(x_ref, sc_ref, sh_ref, g_ref, wa_ref, wb_ref, o_ref, h_scr):
    @pl.when(pl.program_id(1) == 0)
    def _():
        h = _rms(x_ref[...], g_ref[...]) * (1.0 + sc_ref[0]) + sh_ref[0]
        h_scr[...] = h.astype(BF16)

    half = wa_ref.shape[1]
    o_ref[:, :half] = _dot(h_scr[...], wa_ref[...])
    o_ref[:, half:] = _dot(h_scr[...], wb_ref[...])


def _in_proj(x, mod3, g1, w_in_p, st):
    n, d = x.shape
    tm, rpg, r = st["tm_in"], st["rpg"], st["r"]
    tn = 1024
    width = w_in_p.shape[1]

    def gi(i):
        return (i * tm) // rpg

    return pl.pallas_call(
        _in_proj_kernel,
        grid=(n // tm, width // tn),
        in_specs=[
            pl.BlockSpec((tm, d), lambda i, j: (i, 0)),
            pl.BlockSpec((1, r, d), lambda i, j: (gi(i), 0, 1)),
            pl.BlockSpec((1, r, d), lambda i, j: (gi(i), 0, 0)),
            pl.BlockSpec((1, d), lambda i, j: (0, 0)),
            pl.BlockSpec((d, tn // 2), lambda i, j: (0, 2 * j)),
            pl.BlockSpec((d, tn // 2), lambda i, j: (0, 2 * j + 1)),
        ],
        out_specs=pl.BlockSpec((tm, tn), lambda i, j: (i, j)),
        out_shape=jax.ShapeDtypeStruct((n, width), F32),
        scratch_shapes=[pltpu.VMEM((tm, d), BF16)],
        compiler_params=_cparams(("parallel", "arbitrary")),
        name="in_proj",
    )(x, mod3, mod3, g1, w_in_p, w_in_p)


def _mla_prep_kernel(z_ref, gq_ref, gkv_ref, wn_ref, wr_ref, wrs_ref, wuk_ref, cos_ref, sin_ref,
                     q_ref, ckv_ref, kr_ref, kvb_ref):
    z = z_ref[...]
    cq = z[:, :MLA_Q_RANK]
    ckv = z[:, MLA_Q_RANK:MLA_Q_RANK + MLA_KV_RANK]
    o = MLA_Q_RANK + MLA_KV_RANK
    kr = z[:, o:o + MLA_ROPE]
    krs = z[:, o + MLA_ROPE:o + 2 * MLA_ROPE]
    cos = cos_ref[...]
    sin = sin_ref[...]

    cqn = _rms(cq, gq_ref[...]).astype(BF16)
    qn = _dot(cqn, wn_ref[...])
    qrope = (_dot(cqn, wr_ref[...]) * cos + _dot(cqn, wrs_ref[...]) * sin) * MLA_SCALE
    tm = z.shape[0]
    zpad = jnp.zeros((tm, QK_PAD - MLA_KV_RANK - MLA_ROPE), BF16)
    for h in range(MLA_HEADS):
        ql = _dot(qn[:, h * MLA_NOPE:(h + 1) * MLA_NOPE].astype(BF16), wuk_ref[h]) * MLA_SCALE
        q_ref[h, :, 0:MLA_KV_RANK] = ql.astype(BF16)
        q_ref[h, :, MLA_KV_RANK:MLA_KV_RANK + MLA_ROPE] = qrope[:, h * MLA_ROPE:(h + 1) * MLA_ROPE].astype(BF16)
        q_ref[h, :, MLA_KV_RANK + MLA_ROPE:QK_PAD] = zpad

    ckvn = _rms(ckv, gkv_ref[...])
    kro = kr * cos[:, :MLA_ROPE] + krs * sin[:, :MLA_ROPE]
    ckv_ref[...] = ckvn
    kr_ref[...] = kro
    kvb_ref[:, 0:MLA_KV_RANK] = ckvn.astype(BF16)
    kvb_ref[:, MLA_KV_RANK:MLA_KV_RANK + MLA_ROPE] = kro.astype(BF16)
    kvb_ref[:, MLA_KV_RANK + MLA_ROPE:QK_PAD] = zpad


def _mla_prep(z, lw, cos_t, sin_t, st):
    n = z.shape[0]
    tm = st["tm"]
    rt = cos_t.shape[0]
    hr = MLA_HEADS * MLA_ROPE
    if rt == 1:
        tspec = pl.BlockSpec((1, hr), lambda i: (0, 0))
    else:
        nt = rt // tm
        tspec = pl.BlockSpec((tm, hr), lambda i: (i % nt, 0))
    return pl.pallas_call(
        _mla_prep_kernel,
        grid=(n // tm,),
        in_specs=[
            pl.BlockSpec((tm, 1024), lambda i: (i, COL_A // 1024)),
            _const_spec((1, MLA_Q_RANK)),
            _const_spec((1, MLA_KV_RANK)),
            _const_spec((MLA_Q_RANK, MLA_HEADS * MLA_NOPE)),
            _const_spec((MLA_Q_RANK, hr)),
            _const_spec((MLA_Q_RANK, hr)),
            _const_spec((MLA_HEADS, MLA_NOPE, MLA_KV_RANK)),
            tspec,
            tspec,
        ],
        out_specs=[
            pl.BlockSpec((MLA_HEADS, tm, QK_PAD), lambda i: (0, i, 0)),
            pl.BlockSpec((tm, MLA_KV_RANK), lambda i: (i, 0)),
            pl.BlockSpec((tm, MLA_ROPE), lambda i: (i, 0)),
            pl.BlockSpec((tm, QK_PAD), lambda i: (i, 0)),
        ],
        out_shape=[
            jax.ShapeDtypeStruct((MLA_HEADS, n, QK_PAD), BF16),
            jax.ShapeDtypeStruct((n, MLA_KV_RANK), F32),
            jax.ShapeDtypeStruct((n, MLA_ROPE), F32),
            jax.ShapeDtypeStruct((n, QK_PAD), BF16),
        ],
        compiler_params=_cparams(("parallel",)),
        name="mla_prep",
    )(z, lw["g_q"], lw["g_kv"], lw["w_nope"], lw["w_rope"], lw["w_rope_sw"], lw["w_ukT"], cos_t, sin_t)


def _attn_prompt_kernel(q_ref, k_ref, wuv_ref, o_ref, m_scr, l_scr, acc_scr, *, tq, tk):
    i = pl.program_id(1)
    j = pl.program_id(2)
    nh = MLA_HEADS

    @pl.when(j == 0)
    def _():
        m_scr[...] = jnp.full(m_scr.shape, -jnp.inf, F32)
        l_scr[...] = jnp.zeros(l_scr.shape, F32)
        acc_scr[...] = jnp.zeros(acc_scr.shape, F32)

    def block(masked):
        hb = ATT_HB
        k = k_ref[...]
        v = k[:, :MLA_KV_RANK]
        if masked:
            qpos = i * tq + lax.broadcasted_iota(jnp.int32, (tq, tk), 0)
            kpos = j * tk + lax.broadcasted_iota(jnp.int32, (tq, tk), 1)
            keep = jnp.concatenate([kpos <= qpos] * hb, axis=0)
        for h in range(0, nh, hb):
            hs = slice(h, h + hb)
            s = _dot_nt(q_ref[hs].reshape(hb * tq, QK_PAD), k)
            if masked:
                s = jnp.where(keep, s, -jnp.inf)
            m_prev = m_scr[hs].reshape(hb * tq, 1)
            m_new = jnp.maximum(m_prev, jnp.max(s, axis=-1, keepdims=True))
            corr = jnp.exp(m_prev - m_new)
            p = jnp.exp(s - m_new)
            l_new = l_scr[hs].reshape(hb * tq, 1) * corr + jnp.sum(p, axis=-1, keepdims=True)
            a_new = acc_scr[hs].reshape(hb * tq, MLA_KV_RANK) * corr + _dot(p.astype(BF16), v)
            l_scr[hs] = l_new.reshape(hb, tq, 1)
            acc_scr[hs] = a_new.reshape(hb, tq, MLA_KV_RANK)
            m_scr[hs] = m_new.reshape(hb, tq, 1)

    @pl.when(j * tk + tk - 1 <= i * tq)
    def _():
        block(False)

    @pl.when(jnp.logical_and(j * tk + tk - 1 > i * tq, j * tk <= i * tq + tq - 1))
    def _():
        block(True)

    @pl.when(j == pl.num_programs(2) - 1)
    def _():
        for h in range(nh):
            lat = acc_scr[h] / l_scr[h]
            o_ref[:, h * MLA_V:(h + 1) * MLA_V] = _dot(lat.astype(BF16), wuv_ref[h]).astype(BF16)


def _attn_prompt(q, kvb, wuv, bsz, t):
    n = kvb.shape[0]
    tq = _tile(t, ATT_TQ)
    tk = _tile(t, ATT_TK)
    nq, nk = t // tq, t // tk
    kern = functools.partial(_attn_prompt_kernel, tq=tq, tk=tk)
    return pl.pallas_call(
        kern,
        grid=(bsz, nq, nk),
        in_specs=[
            pl.BlockSpec((MLA_HEADS, tq, QK_PAD), lambda b, i, j: (0, b * nq + i, 0)),
            pl.BlockSpec((tk, QK_PAD), lambda b, i, j: (b * nk + jnp.minimum(j, (i * tq + tq - 1) // tk), 0)),
            _const_spec((MLA_HEADS, MLA_KV_RANK, MLA_V)),
        ],
        out_specs=pl.BlockSpec((tq, MLA_HEADS * MLA_V), lambda b, i, j: (b * nq + i, 0)),
        out_shape=jax.ShapeDtypeStruct((n, MLA_HEADS * MLA_V), BF16),
        scratch_shapes=[
            pltpu.VMEM((MLA_HEADS, tq, 1), F32),
            pltpu.VMEM((MLA_HEADS, tq, 1), F32),
            pltpu.VMEM((MLA_HEADS, tq, MLA_KV_RANK), F32),
        ],
        compiler_params=_cparams(("parallel", "parallel", "arbitrary")),
        name="attn_prompt",
    )(q, kvb, wuv)


def _attn_sample_kernel(pt_ref, q_ref, knew_ref, ckv_hbm, kr_hbm, o_ref,
                        kbuf, rbuf, sem, m_scr, l_scr, acc_scr, *, layer, gp, ng, npg, nsteps):
    b = pl.program_id(0)
    g = pl.program_id(1)
    step = b * ng + g
    slot = step % 2

    def copies(st, sl):
        base = (st // ng) * npg + (st % ng) * gp
        cps = []
        for k in range(gp):
            page = pt_ref[base + k]
            cps.append(pltpu.make_async_copy(ckv_hbm.at[layer, page], kbuf.at[sl, k], sem.at[0, sl]))
            cps.append(pltpu.make_async_copy(kr_hbm.at[layer, page], rbuf.at[sl, k], sem.at[1, sl]))
        return cps

    @pl.when(step == 0)
    def _():
        for c in copies(step, slot):
            c.start()

    @pl.when(step + 1 < nsteps)
    def _():
        for c in copies(step + 1, 1 - slot):
            c.start()

    for c in copies(step, slot):
        c.wait()

    @pl.when(g == 0)
    def _():
        m_scr[...] = jnp.full(m_scr.shape, -jnp.inf, F32)
        l_scr[...] = jnp.zeros(l_scr.shape, F32)
        acc_scr[...] = jnp.zeros(acc_scr.shape, F32)

    q = q_ref[0]
    ps = kbuf.shape[2]
    kc = kbuf[slot].reshape(gp * ps, MLA_KV_RANK).astype(BF16)
    qr = q[:, MLA_KV_RANK:MLA_KV_RANK + MLA_ROPE]
    s_rope = jnp.concatenate([_dot(qr, rbuf[slot, k].astype(BF16)) for k in range(gp)], axis=1)
    s = _dot_nt(q[:, :MLA_KV_RANK], kc) + s_rope
    m_prev = m_scr[...]
    m_new = jnp.maximum(m_prev, jnp.max(s, axis=-1, keepdims=True))
    corr = jnp.exp(m_prev - m_new)
    p = jnp.exp(s - m_new)
    l_scr[...] = l_scr[...] * corr + jnp.sum(p, axis=-1, keepdims=True)
    acc_scr[...] = acc_scr[...] * corr + _dot(p.astype(BF16), kc)
    m_scr[...] = m_new

    @pl.when(g == ng - 1)
    def _():
        knew = knew_ref[0].astype(F32)
        s_new = jnp.sum(q.astype(F32) * knew, axis=-1, keepdims=True)
        m_old = m_scr[...]
        m_fin = jnp.maximum(m_old, s_new)
        c_old = jnp.exp(m_old - m_fin)
        p_new = jnp.exp(s_new - m_fin)
        l_fin = l_scr[...] * c_old + p_new
        acc = acc_scr[...] * c_old + p_new * knew[:, :MLA_KV_RANK]
        o_ref[0] = acc / l_fin


def _attn_sample(q_nh, kvb, cache_ckv, cache_krope_t, page_table, layer):
    n = q_nh.shape[0]
    npg = page_table.shape[1]
    ps = cache_ckv.shape[2]
    gp = min(npg, ATT_GP)
    while npg % gp:
        gp -= 1
    ng = npg // gp
    kern = functools.partial(_attn_sample_kernel, layer=layer, gp=gp, ng=ng, npg=npg, nsteps=n * ng)
    grid_spec = pltpu.PrefetchScalarGridSpec(
        num_scalar_prefetch=1,
        grid=(n, ng),
        in_specs=[
            pl.BlockSpec((1, MLA_HEADS, QK_PAD), lambda b, g, pt: (b, 0, 0)),
            pl.BlockSpec((1, 1, QK_PAD), lambda b, g, pt: (b, 0, 0)),
            pl.BlockSpec(memory_space=pl.ANY),
            pl.BlockSpec(memory_space=pl.ANY),
        ],
        out_specs=pl.BlockSpec((1, MLA_HEADS, MLA_KV_RANK), lambda b, g, pt: (b, 0, 0)),
        scratch_shapes=[
            pltpu.VMEM((2, gp, ps, MLA_KV_RANK), F32),
            pltpu.VMEM((2, gp, MLA_ROPE, ps), F32),
            pltpu.SemaphoreType.DMA((2, 2)),
            pltpu.VMEM((MLA_HEADS, 1), F32),
            pltpu.VMEM((MLA_HEADS, 1), F32),
            pltpu.VMEM((MLA_HEADS, MLA_KV_RANK), F32),
        ],
    )
    return pl.pallas_call(
        kern,
        grid_spec=grid_spec,
        out_shape=jax.ShapeDtypeStruct((n, MLA_HEADS, MLA_KV_RANK), F32),
        compiler_params=_cparams(("arbitrary", "arbitrary")),
        name="attn_sample",
    )(page_table.reshape(-1), q_nh, kvb.reshape(n, 1, QK_PAD), cache_ckv, cache_krope_t)


def _uv_proj_kernel(lat_ref, wuv_ref, o_ref):
    for h in range(MLA_HEADS):
        o_ref[:, h * MLA_V:(h + 1) * MLA_V] = _dot(lat_ref[h].astype(BF16), wuv_ref[h]).astype(BF16)


def _uv_proj(lat_hn, wuv):
    n = lat_hn.shape[1]
    return pl.pallas_call(
        _uv_proj_kernel,
        grid=(1,),
        in_specs=[_const_spec(lat_hn.shape), _const_spec(wuv.shape)],
        out_specs=_const_spec((n, MLA_HEADS * MLA_V)),
        out_shape=jax.ShapeDtypeStruct((n, MLA_HEADS * MLA_V), BF16),
        compiler_params=_cparams(("arbitrary",)),
        name="uv_proj",
    )(lat_hn, wuv)


def _hg_gates(fp, lb_ref):
    lsig = jnp.minimum(fp, 0.0) - jnp.log1p(jnp.exp(-jnp.abs(fp)))
    bterm = lb_ref[1:2, :] + lsig
    loglb = lb_ref[0:1, :]
    logf = jnp.maximum(loglb, bterm) + jnp.log1p(jnp.exp(-jnp.abs(loglb - bterm)))
    kk = lb_ref[2:3, :] * jax.nn.sigmoid(-fp)
    return logf, kk


def _hgrn_prompt_kernel(q_ref, f_ref, i_ref, lb_ref, o_ref, s_ref, st_scr, *, c, sb):
    ci = pl.program_id(1)
    nsb = c // sb
    kd = HG_K

    @pl.when(ci == 0)
    def _():
        st_scr[...] = jnp.zeros(st_scr.shape, F32)

    logf, kk_all = _hg_gates(f_ref[...], lb_ref)
    tril = (lax.broadcasted_iota(jnp.int32, (c, c), 0) >= lax.broadcasted_iota(jnp.int32, (c, c), 1)).astype(BF16)
    l_hi, l_mid, l_lo = _split3(logf)
    cum_all = _dot(tril, l_hi) + _dot(tril, l_mid) + _dot(tril, l_lo)
    q_all = q_ref[...]
    v_all = i_ref[...]
    ones = jnp.ones((kd, kd), BF16)
    tpos = lax.broadcasted_iota(jnp.int32, (nsb, sb, kd), 1)

    for h in range(HG_HEADS):
        hs = slice(h * kd, (h + 1) * kd)
        q = q_all[:, hs]
        k = kk_all[:, hs]
        v = v_all[:, hs]
        cu = cum_all[:, hs]
        s0 = st_scr[h]
        vb = v.astype(BF16)

        o_inter = _dot((q * jnp.exp(cu)).astype(BF16), s0.astype(BF16))
        blocks = [o_inter[i * sb:(i + 1) * sb] for i in range(nsb)]

        for j in range(nsb - 1):
            r0 = (j + 1) * sb
            bj = cu[r0 - 1:r0, :]
            ke = (k[j * sb:r0] * jnp.exp(bj - cu[j * sb:r0])).astype(BF16)
            qe = (q[r0:] * jnp.exp(cu[r0:] - bj)).astype(BF16)
            a = _dot_nt(qe, ke)
            lower = _dot(a.astype(BF16), vb[j * sb:r0])
            for i in range(j + 1, nsb):
                blocks[i] = blocks[i] + lower[(i - j - 1) * sb:(i - j) * sb]

        q4 = q.reshape(nsb, sb, kd)
        k4 = k.reshape(nsb, sb, kd)
        v4 = v.reshape(nsb, sb, kd)
        cu4 = cu.reshape(nsb, sb, kd)
        od = jnp.zeros((nsb, sb, kd), F32)
        for s in range(sb):
            d = cu4 - cu4[:, s:s + 1, :]
            w = jnp.where(tpos >= s, jnp.exp(d), 0.0)
            zz = (q4 * w * k4[:, s:s + 1, :]).reshape(c, kd).astype(BF16)
            r = _dot(zz, ones).reshape(nsb, sb, kd)
            od = od + r * v4[:, s:s + 1, :]

        o_ref[:, hs] = jnp.concatenate(blocks, axis=0) + od.reshape(c, kd)

        c_end = cu[c - 1:c, :]
        kdec = (k * jnp.exp(c_end - cu)).astype(BF16)
        upd = _dot_tn(kdec, vb)
        e_col = jnp.broadcast_to(jnp.exp(c_end), (kd, kd)).T
        st_scr[h] = e_col * s0 + upd

    @pl.when(ci == pl.num_programs(1) - 1)
    def _():
        s_ref[0] = st_scr[...]


def _hgrn_prompt(z, lb3, bsz, t):
    n = z.shape[0]
    c = _tile(t, HG_CHUNK)
    sb = min(HG_SUB, c)
    nc = t // c
    w = HG_HEADS * HG_K
    kern = functools.partial(_hgrn_prompt_kernel, c=c, sb=sb)
    return pl.pallas_call(
        kern,
        grid=(bsz, nc),
        in_specs=[
            pl.BlockSpec((c, w), lambda b, ci: (b * nc + ci, COL_HQ // w)),
            pl.BlockSpec((c, w), lambda b, ci: (b * nc + ci, COL_HF // w)),
            pl.BlockSpec((c, w), lambda b, ci: (b * nc + ci, COL_HI // w)),
            _const_spec((3, w)),
        ],
        out_specs=[
            pl.BlockSpec((c, w), lambda b, ci: (b * nc + ci, 0)),
            pl.BlockSpec((1, HG_HEADS, HG_K, HG_V), lambda b, ci: (b, 0, 0, 0)),
        ],
        out_shape=[
            jax.ShapeDtypeStruct((n, w), F32),
            jax.ShapeDtypeStruct((bsz, HG_HEADS, HG_K, HG_V), F32),
        ],
        scratch_shapes=[pltpu.VMEM((HG_HEADS, HG_K, HG_V), F32)],
        compiler_params=_cparams(("parallel", "arbitrary")),
        name="hgrn_prompt",
    )(z, z, z, lb3)


def _hgrn_sample_kernel(q_ref, f_ref, i_ref, lb_ref, s0_ref, o_ref, s_ref, *, tb):
    kd = HG_K
    logf, kk_all = _hg_gates(f_ref[...], lb_ref)
    f_all = jnp.exp(logf)
    q_all = q_ref[...]
    v_all = i_ref[...]
    for b in range(tb):
        for h in range(HG_HEADS):
            hs = slice(h * kd, (h + 1) * kd)
            e_f = jnp.broadcast_to(f_all[b:b + 1, hs], (kd, kd)).T
            e_k = jnp.broadcast_to(kk_all[b:b + 1, hs], (kd, kd)).T
            s_new = e_f * s0_ref[b, h] + e_k * v_all[b:b + 1, hs]
            s_ref[b, h] = s_new
            qrow = jnp.broadcast_to(q_all[b:b + 1, hs], (8, kd)).astype(BF16)
            o_ref[b:b + 1, hs] = _dot(qrow, s_new.astype(BF16))[0:1]


def _hgrn_sample(z, lb3, state, layer):
    n = z.shape[0]
    tb = 8
    w = HG_HEADS * HG_K
    kern = functools.partial(_hgrn_sample_kernel, tb=tb)
    return pl.pallas_call(
        kern,
        grid=(n // tb,),
        in_specs=[
            pl.BlockSpec((tb, w), lambda i: (i, COL_HQ // w)),
            pl.BlockSpec((tb, w), lambda i: (i, COL_HF // w)),
            pl.BlockSpec((tb, w), lambda i: (i, COL_HI // w)),
            _const_spec((3, w)),
            pl.BlockSpec((None, tb, HG_HEADS, HG_K, HG_V), lambda i: (layer, i, 0, 0, 0)),
        ],
        out_specs=[
            pl.BlockSpec((tb, w), lambda i: (i, 0)),
            pl.BlockSpec((tb, HG_HEADS, HG_K, HG_V), lambda i: (i, 0, 0, 0)),
        ],
        out_shape=[
            jax.ShapeDtypeStruct((n, w), F32),
            jax.ShapeDtypeStruct(state.shape[1:], F32),
        ],
        compiler_params=_cparams(("parallel",)),
        name="hgrn_sample",
    )(z, z, z, lb3, state)


def _merge_kernel(att_ref, or_ref, hg_ref, ga_ref, gb_ref, x_ref, gt1_ref, sc2_ref, sh2_ref,
                  ghg_ref, g2_ref, wa_ref, wb_ref, wo_ref, x1_ref, h2_ref, h2t_ref):
    o_r = or_ref[...]
    hg = hg_ref[...]
    ghg = ghg_ref[...]
    parts = []
    for h in range(HG_HEADS):
        hs = slice(h * HG_V, (h + 1) * HG_V)
        g = hg[:, hs]
        parts.append((_rms(o_r[:, hs], ghg) * (g * jax.nn.sigmoid(g))).astype(BF16))
    orn = jnp.concatenate(parts, axis=-1)
    oa = _dot(att_ref[...], wa_ref[...])
    ob = _dot(orn, wb_ref[...])
    m = jax.nn.sigmoid(ga_ref[...]) * oa + jax.nn.sigmoid(gb_ref[...]) * ob
    y = _dot(m.astype(BF16), wo_ref[...])
    x1 = x_ref[...] + gt1_ref[0] * y
    x1_ref[...] = x1
    h2 = _rms(x1, g2_ref[...]) * (1.0 + sc2_ref[0]) + sh2_ref[0]
    h2_ref[...] = h2.astype(BF16)
    h2t_ref[...] = h2.T.astype(BF16)


def _merge(att, o_r, z, x, mod3, lw, st):
    n, d = x.shape
    tm, rpg, r = min(st["tm"], 256), st["rpg"], st["r"]
    w = HG_HEADS * HG_V

    def gi(i):
        return (i * tm) // rpg

    def mspec(k):
        return pl.BlockSpec((1, r, d), lambda i: (gi(i), 0, k))

    one = pl.Buffered(1)
    return pl.pallas_call(
        _merge_kernel,
        grid=(n // tm,),
        in_specs=[
            pl.BlockSpec((tm, w), lambda i: (i, 0)),
            pl.BlockSpec((tm, w), lambda i: (i, 0)),
            pl.BlockSpec((tm, w), lambda i: (i, COL_HG // w)),
            pl.BlockSpec((tm, d), lambda i: (i, COL_GA // d)),
            pl.BlockSpec((tm, d), lambda i: (i, COL_GB // d)),
            pl.BlockSpec((tm, d), lambda i: (i, 0)),
            mspec(2), mspec(4), mspec(3),
            _const_spec((1, HG_V)),
            _const_spec((1, d)),
            pl.BlockSpec((w, d), lambda i: (0, 0), pipeline_mode=one),
            pl.BlockSpec((w, d), lambda i: (0, 0), pipeline_mode=one),
            pl.BlockSpec((d, d), lambda i: (0, 0), pipeline_mode=one),
        ],
        out_specs=[
            pl.BlockSpec((tm, d), lambda i: (i, 0)),
            pl.BlockSpec((tm, d), lambda i: (i, 0)),
            pl.BlockSpec((d, tm), lambda i: (0, i)),
        ],
        out_shape=[
            jax.ShapeDtypeStruct((n, d), F32),
            jax.ShapeDtypeStruct((n, d), BF16),
            jax.ShapeDtypeStruct((d, n), BF16),
        ],
        compiler_params=_cparams(("parallel",)),
        name="merge",
    )(att, o_r, z, z, z, x, mod3, mod3, mod3, lw["g_hg"], lw["g2"], lw["w_a"], lw["w_b"], lw["w_out"])


def _top_sorted(s, k):
    vals = []
    for _ in range(k):
        m = jnp.max(s, axis=0, keepdims=True)
        vals.append(m)
        s = jnp.where(s == m, -jnp.inf, s)
    return vals


def _router_kernel(h2_ref, wq_ref, k1_ref, k2_ref, thr_ref, a_ref, s2_ref, e2_ref, v2_scr):
    qp = _dot(h2_ref[...], wq_ref[...])
    half = PEER_QDIM // 2
    k1 = k1_ref[...]
    k2 = k2_ref[...]
    topk = PEER_TOPK
    nx = topk + 1
    for h in range(PEER_HEADS):
        qa = qp[:, h * PEER_QDIM:h * PEER_QDIM + half].astype(BF16)
        qb = qp[:, h * PEER_QDIM + half:(h + 1) * PEER_QDIM].astype(BF16)
        s1 = _dot_nt(k1, qa)
        s2 = _dot_nt(k2, qb)
        v1 = _top_sorted(s1, nx)
        v2 = _top_sorted(s2, nx)
        v2_scr[...] = jnp.full(v2_scr.shape, -jnp.inf, F32)
        for a in range(nx):
            v2_scr[a:a + 1, :] = v2[a]
        pieces = []
        for a in range(nx):
            nb = -(-(nx // (a + 1)) // 8) * 8
            pieces.append(v1[a] + v2_scr[0:nb, :])
        cand = jnp.concatenate(pieces, axis=0)
        sc = _top_sorted(cand, nx)
        mid = 0.5 * (sc[topk - 1] + sc[topk])
        zsum = jnp.zeros_like(mid)
        for a in range(topk):
            zsum = zsum + jnp.exp(sc[a] - sc[0])
        thr = mid - s1
        aw = jnp.exp(s1 - v1[0]) / zsum
        e2 = jnp.exp(s2 - v2[0])
        for j in range(thr_ref.shape[0]):
            js = slice(j * LANE, (j + 1) * LANE)
            thr_ref[j, h] = thr[:, js]
            a_ref[j, h] = aw[:, js]
            s2_ref[j, h] = s2[:, js]
            e2_ref[j, h] = e2[:, js]


def _router(h2, lw, st):
    n, d = h2.shape
    tm = min(st["tm"], 256)
    nk = PEER_NKEYS
    half = PEER_QDIM // 2
    oshape = (n // LANE, PEER_HEADS, nk, LANE)
    ospec = pl.BlockSpec((tm // LANE, PEER_HEADS, nk, LANE), lambda i: (i, 0, 0, 0))
    return pl.pallas_call(
        _router_kernel,
        grid=(n // tm,),
        in_specs=[
            pl.BlockSpec((tm, d), lambda i: (i, 0)),
            pl.BlockSpec((d, PEER_HEADS * PEER_QDIM), lambda i: (0, 0), pipeline_mode=pl.Buffered(1)),
            _const_spec((nk, half)),
            _const_spec((nk, half)),
        ],
        out_specs=[ospec, ospec, ospec, ospec],
        out_shape=[jax.ShapeDtypeStruct(oshape, F32)] * 4,
        scratch_shapes=[pltpu.VMEM((-(-(PEER_TOPK + 1) // 8) * 8, tm), F32)],
        compiler_params=_cparams(("parallel",)),
        name="peer_router",
    )(h2, lw["peer_wq"], lw["k1"], lw["k2"])


def _peer_kernel(h2t_ref, *refs, te, final):
    ns = PEER_SPLIT
    u_refs, vt_refs = refs[:ns], refs[ns:2 * ns]
    (thr_ref, a_ref, s2_ref, e2_ref, x1_ref, gt2_ref, gf_ref, o_ref, acc_scr, pre_scr, p_scr) = refs[2 * ns:]
    e = pl.program_id(1)
    ni = te // PEER_NKEYS
    nk = PEER_NKEYS
    d, tb = h2t_ref.shape
    ur, vr = te // ns, d // ns

    @pl.when(e == 0)
    def _():
        acc_scr[...] = jnp.zeros(acc_scr.shape, F32)

    for c in range(ns):
        pre_scr[c * ur:(c + 1) * ur, :] = _dot(u_refs[c][...], h2t_ref[...])
    for t in range(tb // LANE):
        ls = slice(t * LANE, (t + 1) * LANE)
        for ii in range(ni):
            w = None
            for h in range(PEER_HEADS):
                thr = thr_ref[t, h, pl.ds(e * ni + ii, 1), :]
                arow = a_ref[t, h, pl.ds(e * ni + ii, 1), :]
                wh = jnp.where(s2_ref[t, h] >= thr, arow * e2_ref[t, h], 0.0)
                w = wh if w is None else w + wh
            pre = pre_scr[ii * nk:(ii + 1) * nk, ls]
            act = 0.5 * pre * (1.0 + lax.erf(pre * (2.0 ** -0.5)))
            p_scr[ii * nk:(ii + 1) * nk, ls] = (w * act).astype(BF16)
    for c in range(ns):
        acc_scr[c * vr:(c + 1) * vr, :] += _dot(vt_refs[c][...], p_scr[...])

    @pl.when(e == pl.num_programs(1) - 1)
    def _():
        x2 = x1_ref[...] + gt2_ref[0] * acc_scr[...].T
        o_ref[...] = _rms(x2, gf_ref[...]) if final else x2


def _peer(h2t, rt, x1, mod3, u_all, vt_all, layer, g_final, st, final):
    n, d = x1.shape
    tb, rpg, r = st["tb"], st["rpg"], st["r"]
    ne = u_all.shape[1]
    te = PEER_TE
    nk = PEER_NKEYS

    def gi(i):
        return (i * tb) // rpg

    one = pl.Buffered(1)
    sspec = pl.BlockSpec((tb // LANE, PEER_HEADS, nk, LANE), lambda i, e: (i, 0, 0, 0), pipeline_mode=one)
    kern = functools.partial(_peer_kernel, te=te, final=final)
    ns = PEER_SPLIT
    u_specs = [pl.BlockSpec((None, te // ns, d), lambda i, e, c=c: (layer, e * ns + c, 0)) for c in range(ns)]
    vt_specs = [pl.BlockSpec((None, None, d // ns, te), lambda i, e, c=c: (layer, e, c, 0)) for c in range(ns)]
    return pl.pallas_call(
        kern,
        grid=(n // tb, ne // te),
        in_specs=[
            pl.BlockSpec((d, tb), lambda i, e: (0, i), pipeline_mode=one),
            *u_specs,
            *vt_specs,
            sspec, sspec, sspec, sspec,
            pl.BlockSpec((tb, d), lambda i, e: (i, 0), pipeline_mode=one),
            pl.BlockSpec((1, r, d), lambda i, e: (gi(i), 0, 5)),
            pl.BlockSpec((1, d), lambda i, e: (0, 0)),
        ],
        out_specs=pl.BlockSpec((tb, d), lambda i, e: (i, 0)),
        out_shape=jax.ShapeDtypeStruct((n, d), F32),
        scratch_shapes=[pltpu.VMEM((d, tb), F32), pltpu.VMEM((te, tb), F32), pltpu.VMEM((te, tb), BF16)],
        compiler_params=_cparams(("parallel", "arbitrary")),
        name="peer_dense",
    )(h2t, *([u_all] * ns), *([vt_all] * ns), rt[0], rt[1], rt[2], rt[3], x1, mod3, g_final)


def _prep_layer(l, w_in, g_norm1, g_qnorm, w_uq, g_kvnorm, w_uk, w_uv, w_a, lbs, g_hg_onorm, w_b,
                w_out, g_norm2, peer_wq, peer_k1, peer_k2, peer_u, peer_v):
    d = w_in.shape[1]
    wi = w_in[l]
    o_kr = MLA_Q_RANK + MLA_KV_RANK
    o_hq = o_kr + MLA_ROPE
    hw = HG_HEADS * HG_K
    o_ga = o_hq + 4 * hw
    assert (o_hq, hw, o_ga + 2 * d, d) == (832, 1024, 9024, 2048)
    kr = wi[:, o_kr:o_hq]
    half = MLA_ROPE // 2
    kr_sw = jnp.concatenate([kr[:, half:], kr[:, :half]], axis=1)
    a_pad = jnp.zeros((d, COL_HQ - COL_A - o_hq - MLA_ROPE), F32)
    w_in_p = jnp.concatenate(
        [wi[:, o_ga:o_ga + d], wi[:, o_ga + d:o_ga + 2 * d], wi[:, :o_hq], kr_sw, a_pad, wi[:, o_hq:o_ga]],
        axis=1).astype(BF16)
    uq = w_uq[l].reshape(MLA_Q_RANK, MLA_HEADS, MLA_NOPE + MLA_ROPE)
    w_nope = uq[:, :, :MLA_NOPE].reshape(MLA_Q_RANK, MLA_HEADS * MLA_NOPE)
    rope = uq[:, :, MLA_NOPE:]
    rope_sw = jnp.concatenate([rope[..., half:], rope[..., :half]], axis=-1)
    lb = lbs[l].reshape(1, hw)
    lb3 = jnp.concatenate([jnp.log(lb), jnp.log1p(-lb), 1.0 - lb], axis=0)
    return {
        "w_in_p": w_in_p,
        "g1": g_norm1[l].reshape(1, d),
        "g_q": g_qnorm[l].reshape(1, MLA_Q_RANK),
        "g_kv": g_kvnorm[l].reshape(1, MLA_KV_RANK),
        "w_nope": w_nope.astype(BF16),
        "w_rope": rope.reshape(MLA_Q_RANK, MLA_HEADS * MLA_ROPE).astype(BF16),
        "w_rope_sw": rope_sw.reshape(MLA_Q_RANK, MLA_HEADS * MLA_ROPE).astype(BF16),
        "w_ukT": jnp.transpose(w_uk[l], (1, 2, 0)).astype(BF16),
        "w_uv": jnp.transpose(w_uv[l], (1, 0, 2)).astype(BF16),
        "w_a": w_a[l].astype(BF16),
        "w_b": w_b[l].astype(BF16),
        "w_out": w_out[l].astype(BF16),
        "lb3": lb3,
        "g_hg": g_hg_onorm[l].reshape(1, HG_V),
        "g2": g_norm2[l].reshape(1, d),
        "peer_wq": peer_wq[l].astype(BF16),
        "k1": peer_k1[l].astype(BF16),
        "k2": peer_k2[l].astype(BF16),
    }


def _rope_tables(pos):
    half = MLA_ROPE // 2
    freq = ROPE_THETA ** (-jnp.arange(half, dtype=F32) / half)
    ang = pos.astype(F32)[:, None] * freq[None, :]
    cos = jnp.cos(ang)
    sin = jnp.sin(ang)
    cos_t = jnp.tile(jnp.concatenate([cos, cos], axis=1), (1, MLA_HEADS))
    sin_t = jnp.tile(jnp.concatenate([-sin, sin], axis=1), (1, MLA_HEADS))
    return cos_t, sin_t


def kernel(x_prompt, x_sample, cache_ckv, cache_krope, state_hgrn, page_table, c_prompt, c_sample, w_ada, b_ada, g_norm1, w_in, g_qnorm, w_uq, g_kvnorm, w_uk, w_uv, w_a, hg_lb_logits, g_hg_onorm, w_b, w_out, g_norm2, peer_wq, peer_k1, peer_k2, peer_u, peer_v, g_final):
    bsz, t, d = x_prompt.shape
    nsq, ts, _ = x_sample.shape
    assert ts == 1
    depth = w_in.shape[0]
    n_p = bsz * t
    n_s = nsq

    lbs = jnp.cumsum(jax.nn.softmax(hg_lb_logits.astype(F32), axis=0), axis=0)
    lbs = lbs - lbs[0]
    layers = [
        _prep_layer(l, w_in, g_norm1, g_qnorm, w_uq, g_kvnorm, w_uk, w_uv, w_a, lbs, g_hg_onorm, w_b,
                    w_out, g_norm2, peer_wq, peer_k1, peer_k2, peer_u, peer_v)
        for l in range(depth)
    ]
    gf = g_final.reshape(1, d)

    mod = _ada_mod(jnp.concatenate([c_prompt, c_sample], axis=0), w_ada, b_ada)
    cos_p, sin_p = _rope_tables(jnp.arange(t))
    past_len = page_table.shape[1] * cache_ckv.shape[2]
    cos_s, sin_s = _rope_tables(past_len + jnp.arange(1))
    cache_krope_t = jnp.swapaxes(cache_krope, 2, 3)
    u_all = peer_u.astype(BF16)
    ne = peer_v.shape[1]
    vt_all = jnp.swapaxes(peer_v.reshape(depth, ne // PEER_TE, PEER_TE, d), 2, 3).astype(BF16)

    st_p = {"tm": _tile(t, 512), "tm_in": _tile(t, 1024), "tb": _tile(t, 512), "rpg": t, "r": 1}
    st_s = {"tm": n_s, "tm_in": n_s, "tb": n_s, "rpg": n_s, "r": n_s}

    xp = x_prompt.reshape(n_p, d)
    xs = x_sample.reshape(n_s, d)
    ckv_p, kr_p, stt_p, ckv_s, kr_s, stt_s = [], [], [], [], [], []
    for l in range(depth):
        lw = layers[l]
        final = l == depth - 1
        mod_p = mod[l, :bsz].reshape(bsz, 1, 6 * d)
        mod_s = mod[l, bsz:].reshape(1, n_s, 6 * d)

        z = _in_proj(xp, mod_p, lw["g1"], lw["w_in_p"], st_p)
        q, ckv, kr, kvb = _mla_prep(z, lw, cos_p, sin_p, st_p)
        att = _attn_prompt(q, kvb, lw["w_uv"], bsz, t)
        o_r, s_end = _hgrn_prompt(z, lw["lb3"], bsz, t)
        x1, h2, h2t = _merge(att, o_r, z, xp, mod_p, lw, st_p)
        rt = _router(h2, lw, st_p)
        xp = _peer(h2t, rt, x1, mod_p, u_all, vt_all, l, gf, st_p, final)
        ckv_p.append(ckv.reshape(bsz, t, MLA_KV_RANK))
        kr_p.append(kr.reshape(bsz, t, MLA_ROPE))
        stt_p.append(s_end)

        z = _in_proj(xs, mod_s, lw["g1"], lw["w_in_p"], st_s)
        q, ckv, kr, kvb = _mla_prep(z, lw, cos_s, sin_s, st_s)
        lat = _attn_sample(jnp.transpose(q, (1, 0, 2)), kvb, cache_ckv, cache_krope_t, page_table, l)
        att = _uv_proj(jnp.transpose(lat, (1, 0, 2)), lw["w_uv"])
        o_r, s_new = _hgrn_sample(z, lw["lb3"], state_hgrn, l)
        x1, h2, h2t = _merge(att, o_r, z, xs, mod_s, lw, st_s)
        rt = _router(h2, lw, st_s)
        xs = _peer(h2t, rt, x1, mod_s, u_all, vt_all, l, gf, st_s, final)
        ckv_s.append(ckv.reshape(n_s, 1, MLA_KV_RANK))
        kr_s.append(kr.reshape(n_s, 1, MLA_ROPE))
        stt_s.append(s_new)

    return (xp.reshape(bsz, t, d), xs.reshape(n_s, 1, d),
            jnp.stack(ckv_p), jnp.stack(kr_p), jnp.stack(stt_p),
            jnp.stack(ckv_s), jnp.stack(kr_s), jnp.stack(stt_s))
```

```python
import functools

import jax
import jax.numpy as jnp
from jax import lax
from jax.experimental import pallas as pl
from jax.experimental.pallas import tpu as pltpu

F32 = jnp.float32
BF16 = jnp.bfloat16

MLA_HEADS = 8
MLA_Q_RANK = 512
MLA_KV_RANK = 256
MLA_NOPE = 128
MLA_ROPE = 64
MLA_V = 128
MLA_SCALE = (MLA_NOPE + MLA_ROPE) ** -0.5
ROPE_THETA = 10000.0
HG_HEADS = 8
HG_K = 128
HG_V = 128
PEER_HEADS = 8
PEER_NKEYS = 128
PEER_QDIM = 256
PEER_TOPK = 16
EPS = 1e-6

QK_PAD = 384
HG_CHUNK = 64
HG_SUB = 16
VMEM_LIMIT = 56 * 1024 * 1024
ATT_TQ, ATT_TK, ATT_GP = 256, 512, 64
ATT_HB = 4
LANE, SUBLANE = 128, 8
PEER_TE = 1024
PEER_SPLIT = 1
PEER_MM_LANES = 256

COL_GA, COL_GB, COL_A, COL_HQ, COL_HF, COL_HI, COL_HG, IN_PAD = 0, 2048, 4096, 5120, 6144, 7168, 8192, 9216


def _cparams(sem, flags=None):
    return pltpu.CompilerParams(dimension_semantics=sem, vmem_limit_bytes=VMEM_LIMIT, flags=flags)


def _tile(n, pref):
    t = min(n, pref)
    while n % t:
        t -= 8
    return t


def _dot(a, b):
    return jnp.dot(a, b, preferred_element_type=F32)


def _dot_nt(a, b):
    return lax.dot_general(a, b, (((1,), (1,)), ((), ())), preferred_element_type=F32)


def _dot_tn(a, b):
    return lax.dot_general(a, b, (((0,), (0,)), ((), ())), preferred_element_type=F32)


def _split3(x):
    hi = x.astype(BF16)
    r = x - hi.astype(F32)
    mid = r.astype(BF16)
    lo = (r - mid.astype(F32)).astype(BF16)
    return hi, mid, lo


def _rms(x, g):
    return x * lax.rsqrt(jnp.mean(x * x, axis=-1, keepdims=True) + EPS) * g


def _const_spec(shape):
    nd = len(shape)
    return pl.BlockSpec(shape, lambda *_: (0,) * nd)


def _ada_kernel(c_ref, w_ref, b_ref, o_ref):
    c = c_ref[...]
    cs = c * jax.nn.sigmoid(c)
    a_hi, a_mid, _ = _split3(cs)
    w_hi, w_mid, _ = _split3(w_ref[0])
    o_ref[0] = _dot(a_hi, w_hi) + _dot(a_mid, w_hi) + _dot(a_hi, w_mid) + b_ref[0]


def _ada_mod(c_all, w_ada, b_ada):
    depth, d, w6 = w_ada.shape
    r = c_all.shape[0]
    tn = 1024
    return pl.pallas_call(
        _ada_kernel,
        grid=(depth, w6 // tn),
        in_specs=[
            pl.BlockSpec((r, d), lambda l, j: (0, 0)),
            pl.BlockSpec((1, d, tn), lambda l, j: (l, 0, j)),
            pl.BlockSpec((1, 1, tn), lambda l, j: (l, 0, j)),
        ],
        out_specs=pl.BlockSpec((1, r, tn), lambda l, j: (l, 0, j)),
        out_shape=jax.ShapeDtypeStruct((depth, r, w6), F32),
        compiler_params=_cparams(("parallel", "parallel")),
        name="ada_mod",
    )(c_all, w_ada, b_ada.reshape(depth, 1, w6))


def _in_proj_kernel(x_ref, sc_ref, sh_ref, g_ref, wa_ref, wb_ref, o_ref, h_scr):
    @pl.when(pl.program_id(1) == 0)
    def _():
        h = _rms(x_ref[...], g_ref[...]) * (1.0 + sc_ref[0]) + sh_ref[0]
        h_scr[...] = h.astype(BF16)

    half = wa_ref.shape[1]
    o_ref[:, :half] = _dot(h_scr[...], wa_ref[...])
    o_ref[:, half:] = _dot(h_scr[...], wb_ref[...])


def _in_proj(x, mod3, g1, w_in_p, st):
    n, d = x.shape
    tm, rpg, r = st["tm_in"], st["rpg"], st["r"]
    tn = 1024
    width = w_in_p.shape[1]

    def gi(i):
        return (i * tm) // rpg

    return pl.pallas_call(
        _in_proj_kernel,
        grid=(n // tm, width // tn),
        in_specs=[
            pl.BlockSpec((tm, d), lambda i, j: (i, 0)),
            pl.BlockSpec((1, r, d), lambda i, j: (gi(i), 0, 1)),
            pl.BlockSpec((1, r, d), lambda i, j: (gi(i), 0, 0)),
            pl.BlockSpec((1, d), lambda i, j: (0, 0)),
            pl.BlockSpec((d, tn // 2), lambda i, j: (0, 2 * j)),
            pl.BlockSpec((d, tn // 2), lambda i, j: (0, 2 * j + 1)),
        ],
        out_specs=pl.BlockSpec((tm, tn), lambda i, j: (i, j)),
        out_shape=jax.ShapeDtypeStruct((n, width), F32),
        scratch_shapes=[pltpu.VMEM((tm, d), BF16)],
        compiler_params=_cparams(("parallel", "arbitrary")),
        name="in_proj",
    )(x, mod3, mod3, g1, w_in_p, w_in_p)


def _mla_prep_kernel(z_ref, gq_ref, gkv_ref, wn_ref, wr_ref, wrs_ref, wuk_ref, cos_ref, sin_ref,
                     q_ref, ckv_ref, kr_ref, kvb_ref):
    z = z_ref[...]
    cq = z[:, :MLA_Q_RANK]
    ckv = z[:, MLA_Q_RANK:MLA_Q_RANK + MLA_KV_RANK]
    o = MLA_Q_RANK + MLA_KV_RANK
    kr = z[:, o:o + MLA_ROPE]
    krs = z[:, o + MLA_ROPE:o + 2 * MLA_ROPE]
    cos = cos_ref[...]
    sin = sin_ref[...]

    cqn = _rms(cq, gq_ref[...]).astype(BF16)
    qn = _dot(cqn, wn_ref[...])
    qrope = (_dot(cqn, wr_ref[...]) * cos + _dot(cqn, wrs_ref[...]) * sin) * MLA_SCALE
    tm = z.shape[0]
    zpad = jnp.zeros((tm, QK_PAD - MLA_KV_RANK - MLA_ROPE), BF16)
    for h in range(MLA_HEADS):
        ql = _dot(qn[:, h * MLA_NOPE:(h + 1) * MLA_NOPE].astype(BF16), wuk_ref[h]) * MLA_SCALE
        q_ref[h, :, 0:MLA_KV_RANK] = ql.astype(BF16)
        q_ref[h, :, MLA_KV_RANK:MLA_KV_RANK + MLA_ROPE] = qrope[:, h * MLA_ROPE:(h + 1) * MLA_ROPE].astype(BF16)
        q_ref[h, :, MLA_KV_RANK + MLA_ROPE:QK_PAD] = zpad

    ckvn = _rms(ckv, gkv_ref[...])
    kro = kr * cos[:, :MLA_ROPE] + krs * sin[:, :MLA_ROPE]
    ckv_ref[...] = ckvn
    kr_ref[...] = kro
    kvb_ref[:, 0:MLA_KV_RANK] = ckvn.astype(BF16)
    kvb_ref[:, MLA_KV_RANK:MLA_KV_RANK + MLA_ROPE] = kro.astype(BF16)
    kvb_ref[:, MLA_KV_RANK + MLA_ROPE:QK_PAD] = zpad


def _mla_prep(z, lw, cos_t, sin_t, st):
    n = z.shape[0]
    tm = st["tm"]
    rt = cos_t.shape[0]
    hr = MLA_HEADS * MLA_ROPE
    if rt == 1:
        tspec = pl.BlockSpec((1, hr), lambda i: (0, 0))
    else:
        nt = rt // tm
        tspec = pl.BlockSpec((tm, hr), lambda i: (i % nt, 0))
    return pl.pallas_call(
        _mla_prep_kernel,
        grid=(n // tm,),
        in_specs=[
            pl.BlockSpec((tm, 1024), lambda i: (i, COL_A // 1024)),
            _const_spec((1, MLA_Q_RANK)),
            _const_spec((1, MLA_KV_RANK)),
            _const_spec((MLA_Q_RANK, MLA_HEADS * MLA_NOPE)),
            _const_spec((MLA_Q_RANK, hr)),
            _const_spec((MLA_Q_RANK, hr)),
            _const_spec((MLA_HEADS, MLA_NOPE, MLA_KV_RANK)),
            tspec,
            tspec,
        ],
        out_specs=[
            pl.BlockSpec((MLA_HEADS, tm, QK_PAD), lambda i: (0, i, 0)),
            pl.BlockSpec((tm, MLA_KV_RANK), lambda i: (i, 0)),
            pl.BlockSpec((tm, MLA_ROPE), lambda i: (i, 0)),
            pl.BlockSpec((tm, QK_PAD), lambda i: (i, 0)),
        ],
        out_shape=[
            jax.ShapeDtypeStruct((MLA_HEADS, n, QK_PAD), BF16),
            jax.ShapeDtypeStruct((n, MLA_KV_RANK), F32),
            jax.ShapeDtypeStruct((n, MLA_ROPE), F32),
            jax.ShapeDtypeStruct((n, QK_PAD), BF16),
        ],
        compiler_params=_cparams(("parallel",)),
        name="mla_prep",
    )(z, lw["g_q"], lw["g_kv"], lw["w_nope"], lw["w_rope"], lw["w_rope_sw"], lw["w_ukT"], cos_t, sin_t)


def _attn_prompt_kernel(q_ref, k_ref, wuv_ref, o_ref, m_scr, l_scr, acc_scr, *, tq, tk):
    i = pl.program_id(1)
    j = pl.program_id(2)
    nh = MLA_HEADS

    @pl.when(j == 0)
    def _():
        m_scr[...] = jnp.full(m_scr.shape, -jnp.inf, F32)
        l_scr[...] = jnp.zeros(l_scr.shape, F32)
        acc_scr[...] = jnp.zeros(acc_scr.shape, F32)

    def block(masked):
        hb = ATT_HB
        k = k_ref[...]
        v = k[:, :MLA_KV_RANK]
        if masked:
            qpos = i * tq + lax.broadcasted_iota(jnp.int32, (tq, tk), 0)
            kpos = j * tk + lax.broadcasted_iota(jnp.int32, (tq, tk), 1)
            keep = jnp.concatenate([kpos <= qpos] * hb, axis=0)
        for h in range(0, nh, hb):
            hs = slice(h, h + hb)
            s = _dot_nt(q_ref[hs].reshape(hb * tq, QK_PAD), k)
            if masked:
                s = jnp.where(keep, s, -jnp.inf)
            m_prev = m_scr[hs].reshape(hb * tq, 1)
            m_new = jnp.maximum(m_prev, jnp.max(s, axis=-1, keepdims=True))
            corr = jnp.exp(m_prev - m_new)
            p = jnp.exp(s - m_new)
            l_new = l_scr[hs].reshape(hb * tq, 1) * corr + jnp.sum(p, axis=-1, keepdims=True)
            a_new = acc_scr[hs].reshape(hb * tq, MLA_KV_RANK) * corr + _dot(p.astype(BF16), v)
            l_scr[hs] = l_new.reshape(hb, tq, 1)
            acc_scr[hs] = a_new.reshape(hb, tq, MLA_KV_RANK)
            m_scr[hs] = m_new.reshape(hb, tq, 1)

    @pl.when(j * tk + tk - 1 <= i * tq)
    def _():
        block(False)

    @pl.when(jnp.logical_and(j * tk + tk - 1 > i * tq, j * tk <= i * tq + tq - 1))
    def _():
        block(True)

    @pl.when(j == pl.num_programs(2) - 1)
    def _():
        for h in range(nh):
            lat = acc_scr[h] / l_scr[h]
            o_ref[:, h * MLA_V:(h + 1) * MLA_V] = _dot(lat.astype(BF16), wuv_ref[h]).astype(BF16)


def _attn_prompt(q, kvb, wuv, bsz, t):
    n = kvb.shape[0]
    tq = _tile(t, ATT_TQ)
    tk = _tile(t, ATT_TK)
    nq, nk = t // tq, t // tk
    kern = functools.partial(_attn_prompt_kernel, tq=tq, tk=tk)
    return pl.pallas_call(
        kern,
        grid=(bsz, nq, nk),
        in_specs=[
            pl.BlockSpec((MLA_HEADS, tq, QK_PAD), lambda b, i, j: (0, b * nq + i, 0)),
            pl.BlockSpec((tk, QK_PAD), lambda b, i, j: (b * nk + jnp.minimum(j, (i * tq + tq - 1) // tk), 0)),
            _const_spec((MLA_HEADS, MLA_KV_RANK, MLA_V)),
        ],
        out_specs=pl.BlockSpec((tq, MLA_HEADS * MLA_V), lambda b, i, j: (b * nq + i, 0)),
        out_shape=jax.ShapeDtypeStruct((n, MLA_HEADS * MLA_V), BF16),
        scratch_shapes=[
            pltpu.VMEM((MLA_HEADS, tq, 1), F32),
            pltpu.VMEM((MLA_HEADS, tq, 1), F32),
            pltpu.VMEM((MLA_HEADS, tq, MLA_KV_RANK), F32),
        ],
        compiler_params=_cparams(("parallel", "parallel", "arbitrary")),
        name="attn_prompt",
    )(q, kvb, wuv)


def _attn_sample_kernel(pt_ref, q_ref, knew_ref, ckv_hbm, kr_hbm, o_ref,
                        kbuf, rbuf, sem, m_scr, l_scr, acc_scr, *, layer, gp, ng, npg, nsteps):
    b = pl.program_id(0)
    g = pl.program_id(1)
    step = b * ng + g
    slot = step % 2

    def copies(st, sl):
        base = (st // ng) * npg + (st % ng) * gp
        cps = []
        for k in range(gp):
            page = pt_ref[base + k]
            cps.append(pltpu.make_async_copy(ckv_hbm.at[layer, page], kbuf.at[sl, k], sem.at[0, sl]))
            cps.append(pltpu.make_async_copy(kr_hbm.at[layer, page], rbuf.at[sl, k], sem.at[1, sl]))
        return cps

    @pl.when(step == 0)
    def _():
        for c in copies(step, slot):
            c.start()

    @pl.when(step + 1 < nsteps)
    def _():
        for c in copies(step + 1, 1 - slot):
            c.start()

    for c in copies(step, slot):
        c.wait()

    @pl.when(g == 0)
    def _():
        m_scr[...] = jnp.full(m_scr.shape, -jnp.inf, F32)
        l_scr[...] = jnp.zeros(l_scr.shape, F32)
        acc_scr[...] = jnp.zeros(acc_scr.shape, F32)

    q = q_ref[0]
    ps = kbuf.shape[2]
    kc = kbuf[slot].reshape(gp * ps, MLA_KV_RANK).astype(BF16)
    qr = q[:, MLA_KV_RANK:MLA_KV_RANK + MLA_ROPE]
    s_rope = jnp.concatenate([_dot(qr, rbuf[slot, k].astype(BF16)) for k in range(gp)], axis=1)
    s = _dot_nt(q[:, :MLA_KV_RANK], kc) + s_rope
    m_prev = m_scr[...]
    m_new = jnp.maximum(m_prev, jnp.max(s, axis=-1, keepdims=True))
    corr = jnp.exp(m_prev - m_new)
    p = jnp.exp(s - m_new)
    l_scr[...] = l_scr[...] * corr + jnp.sum(p, axis=-1, keepdims=True)
    acc_scr[...] = acc_scr[...] * corr + _dot(p.astype(BF16), kc)
    m_scr[...] = m_new

    @pl.when(g == ng - 1)
    def _():
        knew = knew_ref[0].astype(F32)
        s_new = jnp.sum(q.astype(F32) * knew, axis=-1, keepdims=True)
        m_old = m_scr[...]
        m_fin = jnp.maximum(m_old, s_new)
        c_old = jnp.exp(m_old - m_fin)
        p_new = jnp.exp(s_new - m_fin)
        l_fin = l_scr[...] * c_old + p_new
        acc = acc_scr[...] * c_old + p_new * knew[:, :MLA_KV_RANK]
        o_ref[0] = acc / l_fin


def _attn_sample(q_nh, kvb, cache_ckv, cache_krope_t, page_table, layer):
    n = q_nh.shape[0]
    npg = page_table.shape[1]
    ps = cache_ckv.shape[2]
    gp = min(npg, ATT_GP)
    while npg % gp:
        gp -= 1
    ng = npg // gp
    kern = functools.partial(_attn_sample_kernel, layer=layer, gp=gp, ng=ng, npg=npg, nsteps=n * ng)
    grid_spec = pltpu.PrefetchScalarGridSpec(
        num_scalar_prefetch=1,
        grid=(n, ng),
        in_specs=[
            pl.BlockSpec((1, MLA_HEADS, QK_PAD), lambda b, g, pt: (b, 0, 0)),
            pl.BlockSpec((1, 1, QK_PAD), lambda b, g, pt: (b, 0, 0)),
            pl.BlockSpec(memory_space=pl.ANY),
            pl.BlockSpec(memory_space=pl.ANY),
        ],
        out_specs=pl.BlockSpec((1, MLA_HEADS, MLA_KV_RANK), lambda b, g, pt: (b, 0, 0)),
        scratch_shapes=[
            pltpu.VMEM((2, gp, ps, MLA_KV_RANK), F32),
            pltpu.VMEM((2, gp, MLA_ROPE, ps), F32),
            pltpu.SemaphoreType.DMA((2, 2)),
            pltpu.VMEM((MLA_HEADS, 1), F32),
            pltpu.VMEM((MLA_HEADS, 1), F32),
            pltpu.VMEM((MLA_HEADS, MLA_KV_RANK), F32),
        ],
    )
    return pl.pallas_call(
        kern,
        grid_spec=grid_spec,
        out_shape=jax.ShapeDtypeStruct((n, MLA_HEADS, MLA_KV_RANK), F32),
        compiler_params=_cparams(("arbitrary", "arbitrary")),
        name="attn_sample",
    )(page_table.reshape(-1), q_nh, kvb.reshape(n, 1, QK_PAD), cache_ckv, cache_krope_t)


def _uv_proj_kernel(lat_ref, wuv_ref, o_ref):
    for h in range(MLA_HEADS):
        o_ref[:, h * MLA_V:(h + 1) * MLA_V] = _dot(lat_ref[h].astype(BF16), wuv_ref[h]).astype(BF16)


def _uv_proj(lat_hn, wuv):
    n = lat_hn.shape[1]
    return pl.pallas_call(
        _uv_proj_kernel,
        grid=(1,),
        in_specs=[_const_spec(lat_hn.shape), _const_spec(wuv.shape)],
        out_specs=_const_spec((n, MLA_HEADS * MLA_V)),
        out_shape=jax.ShapeDtypeStruct((n, MLA_HEADS * MLA_V), BF16),
        compiler_params=_cparams(("arbitrary",)),
        name="uv_proj",
    )(lat_hn, wuv)


def _hg_gates(fp, lb_ref):
    lsig = jnp.minimum(fp, 0.0) - jnp.log1p(jnp.exp(-jnp.abs(fp)))
    bterm = lb_ref[1:2, :] + lsig
    loglb = lb_ref[0:1, :]
    logf = jnp.maximum(loglb, bterm) + jnp.log1p(jnp.exp(-jnp.abs(loglb - bterm)))
    kk = lb_ref[2:3, :] * jax.nn.sigmoid(-fp)
    return logf, kk


def _hgrn_prompt_kernel(q_ref, f_ref, i_ref, lb_ref, o_ref, s_ref, st_scr, *, c, sb):
    ci = pl.program_id(1)
    nsb = c // sb
    kd = HG_K

    @pl.when(ci == 0)
    def _():
        st_scr[...] = jnp.zeros(st_scr.shape, F32)

    logf, kk_all = _hg_gates(f_ref[...], lb_ref)
    tril = (lax.broadcasted_iota(jnp.int32, (c, c), 0) >= lax.broadcasted_iota(jnp.int32, (c, c), 1)).astype(BF16)
    l_hi, l_mid, l_lo = _split3(logf)
    cum_all = _dot(tril, l_hi) + _dot(tril, l_mid) + _dot(tril, l_lo)
    q_all = q_ref[...]
    v_all = i_ref[...]
    ones = jnp.ones((kd, kd), BF16)
    tpos = lax.broadcasted_iota(jnp.int32, (nsb, sb, kd), 1)

    for h in range(HG_HEADS):
        hs = slice(h * kd, (h + 1) * kd)
        q = q_all[:, hs]
        k = kk_all[:, hs]
        v = v_all[:, hs]
        cu = cum_all[:, hs]
        s0 = st_scr[h]
        vb = v.astype(BF16)

        o_inter = _dot((q * jnp.exp(cu)).astype(BF16), s0.astype(BF16))
        blocks = [o_inter[i * sb:(i + 1) * sb] for i in range(nsb)]

        for j in range(nsb - 1):
            r0 = (j + 1) * sb
            bj = cu[r0 - 1:r0, :]
            ke = (k[j * sb:r0] * jnp.exp(bj - cu[j * sb:r0])).astype(BF16)
            qe = (q[r0:] * jnp.exp(cu[r0:] - bj)).astype(BF16)
            a = _dot_nt(qe, ke)
            lower = _dot(a.astype(BF16), vb[j * sb:r0])
            for i in range(j + 1, nsb):
                blocks[i] = blocks[i] + lower[(i - j - 1) * sb:(i - j) * sb]

        q4 = q.reshape(nsb, sb, kd)
        k4 = k.reshape(nsb, sb, kd)
        v4 = v.reshape(nsb, sb, kd)
        cu4 = cu.reshape(nsb, sb, kd)
        hf = sb // 2 if sb % (2 * SUBLANE) == 0 else 0
        od_lo = jnp.zeros((nsb, sb - hf, kd), F32) if hf == 0 else jnp.zeros((nsb, hf, kd), F32)
        od_hi = jnp.zeros((nsb, sb - hf, kd), F32)
        for s in range(sb):
            lo = hf if (hf and s >= hf) else 0
            d = cu4[:, lo:, :] - cu4[:, s:s + 1, :]
            w = jnp.where(tpos[:, lo:, :] >= s, jnp.exp(d), 0.0)
            zz = (q4[:, lo:, :] * w * k4[:, s:s + 1, :]).reshape(nsb * (sb - lo), kd).astype(BF16)
            r = _dot(zz, ones).reshape(nsb, sb - lo, kd) * v4[:, s:s + 1, :]
            if hf == 0:
                od_hi = od_hi + r
            elif lo == 0:
                od_lo = od_lo + r[:, :hf, :]
                od_hi = od_hi + r[:, hf:, :]
            else:
                od_hi = od_hi + r
        od = od_hi if hf == 0 else jnp.concatenate([od_lo, od_hi], axis=1)

        o_ref[:, hs] = jnp.concatenate(blocks, axis=0) + od.reshape(c, kd)

        c_end = cu[c - 1:c, :]
        kdec = (k * jnp.exp(c_end - cu)).astype(BF16)
        upd = _dot_tn(kdec, vb)
        e_col = jnp.broadcast_to(jnp.exp(c_end), (kd, kd)).T
        st_scr[h] = e_col * s0 + upd

    @pl.when(ci == pl.num_programs(1) - 1)
    def _():
        s_ref[0] = st_scr[...]


def _hgrn_prompt(z, lb3, bsz, t):
    n = z.shape[0]
    c = _tile(t, HG_CHUNK)
    sb = min(HG_SUB, c)
    nc = t // c
    w = HG_HEADS * HG_K
    kern = functools.partial(_hgrn_prompt_kernel, c=c, sb=sb)
    return pl.pallas_call(
        kern,
        grid=(bsz, nc),
        in_specs=[
            pl.BlockSpec((c, w), lambda b, ci: (b * nc + ci, COL_HQ // w)),
            pl.BlockSpec((c, w), lambda b, ci: (b * nc + ci, COL_HF // w)),
            pl.BlockSpec((c, w), lambda b, ci: (b * nc + ci, COL_HI // w)),
            _const_spec((3, w)),
        ],
        out_specs=[
            pl.BlockSpec((c, w), lambda b, ci: (b * nc + ci, 0)),
            pl.BlockSpec((1, HG_HEADS, HG_K, HG_V), lambda b, ci: (b, 0, 0, 0)),
        ],
        out_shape=[
            jax.ShapeDtypeStruct((n, w), F32),
            jax.ShapeDtypeStruct((bsz, HG_HEADS, HG_K, HG_V), F32),
        ],
        scratch_shapes=[pltpu.VMEM((HG_HEADS, HG_K, HG_V), F32)],
        compiler_params=_cparams(("parallel", "arbitrary")),
        name="hgrn_prompt",
    )(z, z, z, lb3)


def _hgrn_sample_kernel(q_ref, f_ref, i_ref, lb_ref, s0_ref, o_ref, s_ref, *, tb):
    kd = HG_K
    logf, kk_all = _hg_gates(f_ref[...], lb_ref)
    f_all = jnp.exp(logf)
    q_all = q_ref[...]
    v_all = i_ref[...]
    for b in range(tb):
        for h in range(HG_HEADS):
            hs = slice(h * kd, (h + 1) * kd)
            e_f = jnp.broadcast_to(f_all[b:b + 1, hs], (kd, kd)).T
            e_k = jnp.broadcast_to(kk_all[b:b + 1, hs], (kd, kd)).T
            s_new = e_f * s0_ref[b, h] + e_k * v_all[b:b + 1, hs]
            s_ref[b, h] = s_new
            qrow = jnp.broadcast_to(q_all[b:b + 1, hs], (8, kd)).astype(BF16)
            o_ref[b:b + 1, hs] = _dot(qrow, s_new.astype(BF16))[0:1]


def _hgrn_sample(z, lb3, state, layer):
    n = z.shape[0]
    tb = 8
    w = HG_HEADS * HG_K
    kern = functools.partial(_hgrn_sample_kernel, tb=tb)
    return pl.pallas_call(
        kern,
        grid=(n // tb,),
        in_specs=[
            pl.BlockSpec((tb, w), lambda i: (i, COL_HQ // w)),
            pl.BlockSpec((tb, w), lambda i: (i, COL_HF // w)),
            pl.BlockSpec((tb, w), lambda i: (i, COL_HI // w)),
            _const_spec((3, w)),
            pl.BlockSpec((None, tb, HG_HEADS, HG_K, HG_V), lambda i: (layer, i, 0, 0, 0)),
        ],
        out_specs=[
            pl.BlockSpec((tb, w), lambda i: (i, 0)),
            pl.BlockSpec((tb, HG_HEADS, HG_K, HG_V), lambda i: (i, 0, 0, 0)),
        ],
        out_shape=[
            jax.ShapeDtypeStruct((n, w), F32),
            jax.ShapeDtypeStruct(state.shape[1:], F32),
        ],
        compiler_params=_cparams(("parallel",)),
        name="hgrn_sample",
    )(z, z, z, lb3, state)


def _merge_kernel(att_ref, or_ref, hg_ref, ga_ref, gb_ref, x_ref, gt1_ref, sc2_ref, sh2_ref,
                  ghg_ref, g2_ref, wa_ref, wb_ref, wo_ref, x1_ref, h2_ref, h2t_ref):
    o_r = or_ref[...]
    hg = hg_ref[...]
    ghg = ghg_ref[...]
    parts = []
    for h in range(HG_HEADS):
        hs = slice(h * HG_V, (h + 1) * HG_V)
        g = hg[:, hs]
        parts.append((_rms(o_r[:, hs], ghg) * (g * jax.nn.sigmoid(g))).astype(BF16))
    orn = jnp.concatenate(parts, axis=-1)
    oa = _dot(att_ref[...], wa_ref[...])
    ob = _dot(orn, wb_ref[...])
    m = jax.nn.sigmoid(ga_ref[...]) * oa + jax.nn.sigmoid(gb_ref[...]) * ob
    y = _dot(m.astype(BF16), wo_ref[...])
    x1 = x_ref[...] + gt1_ref[0] * y
    x1_ref[...] = x1
    h2 = _rms(x1, g2_ref[...]) * (1.0 + sc2_ref[0]) + sh2_ref[0]
    h2_ref[...] = h2.astype(BF16)
    h2t_ref[...] = h2.T.astype(BF16)


def _merge(att, o_r, z, x, mod3, lw, st):
    n, d = x.shape
    tm, rpg, r = min(st["tm"], 256), st["rpg"], st["r"]
    w = HG_HEADS * HG_V

    def gi(i):
        return (i * tm) // rpg

    def mspec(k):
        return pl.BlockSpec((1, r, d), lambda i: (gi(i), 0, k))

    one = pl.Buffered(1)
    return pl.pallas_call(
        _merge_kernel,
        grid=(n // tm,),
        in_specs=[
            pl.BlockSpec((tm, w), lambda i: (i, 0)),
            pl.BlockSpec((tm, w), lambda i: (i, 0)),
            pl.BlockSpec((tm, w), lambda i: (i, COL_HG // w)),
            pl.BlockSpec((tm, d), lambda i: (i, COL_GA // d)),
            pl.BlockSpec((tm, d), lambda i: (i, COL_GB // d)),
            pl.BlockSpec((tm, d), lambda i: (i, 0)),
            mspec(2), mspec(4), mspec(3),
            _const_spec((1, HG_V)),
            _const_spec((1, d)),
            pl.BlockSpec((w, d), lambda i: (0, 0), pipeline_mode=one),
            pl.BlockSpec((w, d), lambda i: (0, 0), pipeline_mode=one),
            pl.BlockSpec((d, d), lambda i: (0, 0), pipeline_mode=one),
        ],
        out_specs=[
            pl.BlockSpec((tm, d), lambda i: (i, 0)),
            pl.BlockSpec((tm, d), lambda i: (i, 0)),
            pl.BlockSpec((d, tm), lambda i: (0, i)),
        ],
        out_shape=[
            jax.ShapeDtypeStruct((n, d), F32),
            jax.ShapeDtypeStruct((n, d), BF16),
            jax.ShapeDtypeStruct((d, n), BF16),
        ],
        compiler_params=_cparams(("parallel",)),
        name="merge",
    )(att, o_r, z, z, z, x, mod3, mod3, mod3, lw["g_hg"], lw["g2"], lw["w_a"], lw["w_b"], lw["w_out"])


def _top_sorted(s, k):
    vals = []
    for _ in range(k):
        m = jnp.max(s, axis=0, keepdims=True)
        vals.append(m)
        s = jnp.where(s == m, -jnp.inf, s)
    return vals


def _router_kernel(h2_ref, wq_ref, k1_ref, k2_ref, ta_ref, se_ref, v2_scr):
    qp = _dot(h2_ref[...], wq_ref[...])
    half = PEER_QDIM // 2
    k1 = k1_ref[...]
    k2 = k2_ref[...]
    topk = PEER_TOPK
    nx = topk + 1
    for h in range(PEER_HEADS):
        qa = qp[:, h * PEER_QDIM:h * PEER_QDIM + half].astype(BF16)
        qb = qp[:, h * PEER_QDIM + half:(h + 1) * PEER_QDIM].astype(BF16)
        s1 = _dot_nt(k1, qa)
        s2 = _dot_nt(k2, qb)
        v1 = _top_sorted(s1, nx)
        v2 = _top_sorted(s2, nx)
        v2_scr[...] = jnp.full(v2_scr.shape, -jnp.inf, F32)
        for a in range(nx):
            v2_scr[a:a + 1, :] = v2[a]
        pieces = []
        for a in range(nx):
            nb = -(-(nx // (a + 1)) // 8) * 8
            pieces.append(v1[a] + v2_scr[0:nb, :])
        cand = jnp.concatenate(pieces, axis=0)
        sc = _top_sorted(cand, nx)
        mid = 0.5 * (sc[topk - 1] + sc[topk])
        zsum = jnp.zeros_like(mid)
        for a in range(topk):
            zsum = zsum + jnp.exp(sc[a] - sc[0])
        thr = mid - s1
        aw = jnp.exp(s1 - v1[0]) / zsum
        e2 = jnp.exp(s2 - v2[0])
        for j in range(ta_ref.shape[0]):
            js = slice(j * LANE, (j + 1) * LANE)
            ta_ref[j, h, :, 0:LANE] = thr[:, js]
            ta_ref[j, h, :, LANE:2 * LANE] = aw[:, js]
            se_ref[j, h, :, 0:LANE] = s2[:, js]
            se_ref[j, h, :, LANE:2 * LANE] = e2[:, js]


def _router(h2, lw, st):
    n, d = h2.shape
    tm = min(st["tm"], 256)
    nk = PEER_NKEYS
    half = PEER_QDIM // 2
    oshape = (n // LANE, PEER_HEADS, nk, 2 * LANE)
    ospec = pl.BlockSpec((tm // LANE, PEER_HEADS, nk, 2 * LANE), lambda i: (i, 0, 0, 0))
    return pl.pallas_call(
        _router_kernel,
        grid=(n // tm,),
        in_specs=[
            pl.BlockSpec((tm, d), lambda i: (i, 0)),
            pl.BlockSpec((d, PEER_HEADS * PEER_QDIM), lambda i: (0, 0), pipeline_mode=pl.Buffered(1)),
            _const_spec((nk, half)),
            _const_spec((nk, half)),
        ],
        out_specs=[ospec, ospec],
        out_shape=[jax.ShapeDtypeStruct(oshape, F32)] * 2,
        scratch_shapes=[pltpu.VMEM((-(-(PEER_TOPK + 1) // 8) * 8, tm), F32)],
        compiler_params=_cparams(("parallel",)),
        name="peer_router",
    )(h2, lw["peer_wq"], lw["k1"], lw["k2"])


def _peer_kernel(h2t_ref, *refs, te, final):
    ns = PEER_SPLIT
    u_refs, vt_refs = refs[:ns], refs[ns:2 * ns]
    (ta_ref, se_ref, x1_ref, gt2_ref, gf_ref, o_ref, acc_scr, pre_scr, p_scr) = refs[2 * ns:]
    e = pl.program_id(1)
    ni = te // PEER_NKEYS
    nk = PEER_NKEYS
    d, tb = h2t_ref.shape
    ur, vr = te // ns, d // ns

    @pl.when(e == 0)
    def _():
        acc_scr[...] = jnp.zeros(acc_scr.shape, F32)

    for c in range(ns):
        pre_scr[c * ur:(c + 1) * ur, :] = _dot(u_refs[c][...], h2t_ref[...])
    for t in range(tb // LANE):
        ls = slice(t * LANE, (t + 1) * LANE)
        for ii in range(ni):
            w = None
            for h in range(PEER_HEADS):
                row = ta_ref[t, h, pl.ds(e * ni + ii, 1), :]
                thr = row[:, 0:LANE]
                arow = row[:, LANE:2 * LANE]
                wh = jnp.where(se_ref[t, h, :, 0:LANE] >= thr, arow * se_ref[t, h, :, LANE:2 * LANE], 0.0)
                w = wh if w is None else w + wh
            pre = pre_scr[ii * nk:(ii + 1) * nk, ls]
            act = 0.5 * pre * (1.0 + lax.erf(pre * (2.0 ** -0.5)))
            p_scr[ii * nk:(ii + 1) * nk, ls] = (w * act).astype(BF16)
    for c in range(ns):
        acc_scr[c * vr:(c + 1) * vr, :] += _dot(vt_refs[c][...], p_scr[...])

    @pl.when(e == pl.num_programs(1) - 1)
    def _():
        x2 = x1_ref[...] + gt2_ref[0] * acc_scr[...].T
        o_ref[...] = _rms(x2, gf_ref[...]) if final else x2


def _peer(h2t, rt, x1, mod3, u_all, vt_all, layer, g_final, st, final):
    n, d = x1.shape
    tb, rpg, r = st["tb"], st["rpg"], st["r"]
    ne = u_all.shape[1]
    te = PEER_TE
    nk = PEER_NKEYS

    def gi(i):
        return (i * tb) // rpg

    one = pl.Buffered(1)
    sspec = pl.BlockSpec((tb // LANE, PEER_HEADS, nk, 2 * LANE), lambda i, e: (i, 0, 0, 0), pipeline_mode=one)
    kern = functools.partial(_peer_kernel, te=te, final=final)
    ns = PEER_SPLIT
    u_specs = [pl.BlockSpec((None, te // ns, d), lambda i, e, c=c: (layer, e * ns + c, 0)) for c in range(ns)]
    vt_specs = [pl.BlockSpec((None, None, d // ns, te), lambda i, e, c=c: (layer, e, c, 0)) for c in range(ns)]
    return pl.pallas_call(
        kern,
        grid=(n // tb, ne // te),
        in_specs=[
            pl.BlockSpec((d, tb), lambda i, e: (0, i), pipeline_mode=one),
            *u_specs,
            *vt_specs,
            sspec, sspec,
            pl.BlockSpec((tb, d), lambda i, e: (i, 0), pipeline_mode=one),
            pl.BlockSpec((1, r, d), lambda i, e: (gi(i), 0, 5)),
            pl.BlockSpec((1, d), lambda i, e: (0, 0)),
        ],
        out_specs=pl.BlockSpec((tb, d), lambda i, e: (i, 0)),
        out_shape=jax.ShapeDtypeStruct((n, d), F32),
        scratch_shapes=[pltpu.VMEM((d, tb), F32), pltpu.VMEM((te, tb), F32), pltpu.VMEM((te, tb), BF16)],
        compiler_params=_cparams(("parallel", "arbitrary")),
        name="peer_dense",
    )(h2t, *([u_all] * ns), *([vt_all] * ns), rt[0], rt[1], x1, mod3, g_final)


def _prep_layer(l, w_in, g_norm1, g_qnorm, w_uq, g_kvnorm, w_uk, w_uv, w_a, lbs, g_hg_onorm, w_b,
                w_out, g_norm2, peer_wq, peer_k1, peer_k2, peer_u, peer_v):
    d = w_in.shape[1]
    wi = w_in[l]
    o_kr = MLA_Q_RANK + MLA_KV_RANK
    o_hq = o_kr + MLA_ROPE
    hw = HG_HEADS * HG_K
    o_ga = o_hq + 4 * hw
    assert (o_hq, hw, o_ga + 2 * d, d) == (832, 1024, 9024, 2048)
    kr = wi[:, o_kr:o_hq]
    half = MLA_ROPE // 2
    kr_sw = jnp.concatenate([kr[:, half:], kr[:, :half]], axis=1)
    a_pad = jnp.zeros((d, COL_HQ - COL_A - o_hq - MLA_ROPE), F32)
    w_in_p = jnp.concatenate(
        [wi[:, o_ga:o_ga + d], wi[:, o_ga + d:o_ga + 2 * d], wi[:, :o_hq], kr_sw, a_pad, wi[:, o_hq:o_ga]],
        axis=1).astype(BF16)
    uq = w_uq[l].reshape(MLA_Q_RANK, MLA_HEADS, MLA_NOPE + MLA_ROPE)
    w_nope = uq[:, :, :MLA_NOPE].reshape(MLA_Q_RANK, MLA_HEADS * MLA_NOPE)
    rope = uq[:, :, MLA_NOPE:]
    rope_sw = jnp.concatenate([rope[..., half:], rope[..., :half]], axis=-1)
    lb = lbs[l].reshape(1, hw)
    lb3 = jnp.concatenate([jnp.log(lb), jnp.log1p(-lb), 1.0 - lb], axis=0)
    return {
        "w_in_p": w_in_p,
        "g1": g_norm1[l].reshape(1, d),
        "g_q": g_qnorm[l].reshape(1, MLA_Q_RANK),
        "g_kv": g_kvnorm[l].reshape(1, MLA_KV_RANK),
        "w_nope": w_nope.astype(BF16),
        "w_rope": rope.reshape(MLA_Q_RANK, MLA_HEADS * MLA_ROPE).astype(BF16),
        "w_rope_sw": rope_sw.reshape(MLA_Q_RANK, MLA_HEADS * MLA_ROPE).astype(BF16),
        "w_ukT": jnp.transpose(w_uk[l], (1, 2, 0)).astype(BF16),
        "w_uv": jnp.transpose(w_uv[l], (1, 0, 2)).astype(BF16),
        "w_a": w_a[l].astype(BF16),
        "w_b": w_b[l].astype(BF16),
        "w_out": w_out[l].astype(BF16),
        "lb3": lb3,
        "g_hg": g_hg_onorm[l].reshape(1, HG_V),
        "g2": g_norm2[l].reshape(1, d),
        "peer_wq": peer_wq[l].astype(BF16),
        "k1": peer_k1[l].astype(BF16),
        "k2": peer_k2[l].astype(BF16),
    }


def _rope_tables(pos):
    half = MLA_ROPE // 2
    freq = ROPE_THETA ** (-jnp.arange(half, dtype=F32) / half)
    ang = pos.astype(F32)[:, None] * freq[None, :]
    cos = jnp.cos(ang)
    sin = jnp.sin(ang)
    cos_t = jnp.tile(jnp.concatenate([cos, cos], axis=1), (1, MLA_HEADS))
    sin_t = jnp.tile(jnp.concatenate([-sin, sin], axis=1), (1, MLA_HEADS))
    return cos_t, sin_t


def kernel(x_prompt, x_sample, cache_ckv, cache_krope, state_hgrn, page_table, c_prompt, c_sample, w_ada, b_ada, g_norm1, w_in, g_qnorm, w_uq, g_kvnorm, w_uk, w_uv, w_a, hg_lb_logits, g_hg_onorm, w_b, w_out, g_norm2, peer_wq, peer_k1, peer_k2, peer_u, peer_v, g_final):
    bsz, t, d = x_prompt.shape
    nsq, ts, _ = x_sample.shape
    assert ts == 1
    depth = w_in.shape[0]
    n_p = bsz * t
    n_s = nsq

    lbs = jnp.cumsum(jax.nn.softmax(hg_lb_logits.astype(F32), axis=0), axis=0)
    lbs = lbs - lbs[0]
    layers = [
        _prep_layer(l, w_in, g_norm1, g_qnorm, w_uq, g_kvnorm, w_uk, w_uv, w_a, lbs, g_hg_onorm, w_b,
                    w_out, g_norm2, peer_wq, peer_k1, peer_k2, peer_u, peer_v)
        for l in range(depth)
    ]
    gf = g_final.reshape(1, d)

    mod = _ada_mod(jnp.concatenate([c_prompt, c_sample], axis=0), w_ada, b_ada)
    cos_p, sin_p = _rope_tables(jnp.arange(t))
    past_len = page_table.shape[1] * cache_ckv.shape[2]
    cos_s, sin_s = _rope_tables(past_len + jnp.arange(1))
    cache_krope_t = jnp.swapaxes(cache_krope, 2, 3)
    u_all = peer_u.astype(BF16)
    ne = peer_v.shape[1]
    vt_all = jnp.swapaxes(peer_v.reshape(depth, ne // PEER_TE, PEER_TE, d), 2, 3).astype(BF16)

    st_p = {"tm": _tile(t, 512), "tm_in": _tile(t, 1024), "tb": _tile(t, 512), "rpg": t, "r": 1}
    st_s = {"tm": n_s, "tm_in": n_s, "tb": n_s, "rpg": n_s, "r": n_s}

    xp = x_prompt.reshape(n_p, d)
    xs = x_sample.reshape(n_s, d)
    ckv_p, kr_p, stt_p, ckv_s, kr_s, stt_s = [], [], [], [], [], []
    for l in range(depth):
        lw = layers[l]
        final = l == depth - 1
        mod_p = mod[l, :bsz].reshape(bsz, 1, 6 * d)
        mod_s = mod[l, bsz:].reshape(1, n_s, 6 * d)

        z = _in_proj(xp, mod_p, lw["g1"], lw["w_in_p"], st_p)
        q, ckv, kr, kvb = _mla_prep(z, lw, cos_p, sin_p, st_p)
        att = _attn_prompt(q, kvb, lw["w_uv"], bsz, t)
        o_r, s_end = _hgrn_prompt(z, lw["lb3"], bsz, t)
        x1, h2, h2t = _merge(att, o_r, z, xp, mod_p, lw, st_p)
        rt = _router(h2, lw, st_p)
        xp = _peer(h2t, rt, x1, mod_p, u_all, vt_all, l, gf, st_p, final)
        ckv_p.append(ckv.reshape(bsz, t, MLA_KV_RANK))
        kr_p.append(kr.reshape(bsz, t, MLA_ROPE))
        stt_p.append(s_end)

        z = _in_proj(xs, mod_s, lw["g1"], lw["w_in_p"], st_s)
        q, ckv, kr, kvb = _mla_prep(z, lw, cos_s, sin_s, st_s)
        lat = _attn_sample(jnp.transpose(q, (1, 0, 2)), kvb, cache_ckv, cache_krope_t, page_table, l)
        att = _uv_proj(jnp.transpose(lat, (1, 0, 2)), lw["w_uv"])
        o_r, s_new = _hgrn_sample(z, lw["lb3"], state_hgrn, l)
        x1, h2, h2t = _merge(att, o_r, z, xs, mod_s, lw, st_s)
        rt = _router(h2, lw, st_s)
        xs = _peer(h2t, rt, x1, mod_s, u_all, vt_all, l, gf, st_s, final)
        ckv_s.append(ckv.reshape(n_s, 1, MLA_KV_RANK))
        kr_s.append(kr.reshape(n_s, 1, MLA_ROPE))
        stt_s.append(s_new)

    return (xp.reshape(bsz, t, d), xs.reshape(n_s, 1, d),
            jnp.stack(ckv_p), jnp.stack(kr_p), jnp.stack(stt_p),
            jnp.stack(ckv_s), jnp.stack(kr_s), jnp.stack(stt_s))
```
